```python
import math
import jax, jax.numpy as jnp
from jax import lax
import numpy as np

D_MODEL = 1024
BATCH = 32
SEQ = 256
DEPTH = 4
DEC_BATCH = 8
DEC_SEQ = 1024
PAST_LEN = 512

GRID_W = 64
N_AB_LAYERS = (DEPTH + 1) // 2
N_C_LAYERS = DEPTH // 2
MIX_WIDTH = D_MODEL
RET_HEADS = 4
RET_V_DIM = (MIX_WIDTH // 2) // RET_HEADS
RET_QK_DIM = RET_V_DIM // 2
RET_CHUNK = 128
SSM_WIDTH = MIX_WIDTH // 2
SSM_GROUP = 16
SSM_GROUPS = SSM_WIDTH // SSM_GROUP
SSM_STATE = 64
MLA_HEADS = 8
MLA_NOPE = 64
MLA_ROPE = 32
MLA_V = MIX_WIDTH // MLA_HEADS
MLA_Q_RANK = 256
MLA_KV_RANK = 128
ATTN_Q_BLOCK = 128
N_EXPERTS = 32
TOP_K = 4
D_EXPERT = D_MODEL
SWIGLU_LIMIT = 7.0
SWIGLU_ALPHA = 1.702
MOE_BLOCK = 128
ROPE_BASE = 10000.0
NORM_EPS = 1e-6
AB_IN_WIDTH = 2 * RET_HEADS * RET_QK_DIM + 2 * RET_HEADS * RET_V_DIM + SSM_WIDTH
C_IN_WIDTH = MLA_Q_RANK + MLA_KV_RANK + MLA_ROPE

kernel_name = 'hybrid_retention_s5_mla_moe_diffusion_step'


def rmsnorm(x, g):
    xf = x.astype(jnp.float32)
    y = xf * lax.rsqrt(jnp.mean(xf * xf, axis=-1, keepdims=True) + NORM_EPS)
    return (y * g.astype(jnp.float32)).astype(x.dtype)


def modulate(h, shift, scale):
    return h * (1 + scale) + shift


def grid_positions(L):
    rows = L // GRID_W
    r, cl = jnp.meshgrid(jnp.arange(rows), jnp.arange(GRID_W), indexing='ij')
    return r.reshape(-1).astype(jnp.float32), cl.reshape(-1).astype(jnp.float32)


def axial_rope(x):
    L, d = x.shape[1], x.shape[-1]
    row, col = grid_positions(L)
    half = d // 2
    nf = half // 2
    inv = ROPE_BASE ** (-jnp.arange(nf, dtype=jnp.float32) / nf)
    xf = x.astype(jnp.float32)

    def rot(xa, pos):
        ang = pos[:, None] * inv[None, :]
        cos = jnp.cos(ang)[None, :, None, :]
        sin = jnp.sin(ang)[None, :, None, :]
        x1, x2 = xa[..., :nf], xa[..., nf:]
        return jnp.concatenate([x1 * cos - x2 * sin, x1 * sin + x2 * cos], axis=-1)

    return jnp.concatenate([rot(xf[..., :half], row), rot(xf[..., half:], col)], axis=-1).astype(x.dtype)


def retention_chunkwise(q, k, v, log_g, s0):
    b, L, H, dk = q.shape
    dv = v.shape[-1]
    C = RET_CHUNK
    n = L // C
    f32 = jnp.float32
    q = q.astype(f32).reshape(b, n, C, H, dk)
    k = k.astype(f32).reshape(b, n, C, H, dk)
    v = v.astype(f32).reshape(b, n, C, H, dv)
    pos = jnp.arange(C, dtype=f32)
    diff = pos[:, None] - pos[None, :]
    inner_decay = jnp.where(diff[None] >= 0,
                            jnp.exp(jnp.maximum(diff, 0.0)[None] * log_g[:, None, None]), 0.0)
    scores = jnp.einsum('bnihd,bnmhd->bnhim', q, k) * inner_decay
    o_inner = jnp.einsum('bnhim,bnmhe->bnihe', scores, v)
    k_dec = k * jnp.exp((C - 1 - pos)[:, None] * log_g[None, :])[..., None]
    kv_chunk = jnp.einsum('bnmhd,bnmhe->nbhde', k_dec, v)
    chunk_decay = jnp.exp(C * log_g)[:, None, None]

    def step(s, kv):
        return chunk_decay * s + kv, s

    s_final, s_prev = lax.scan(step, s0.astype(f32), kv_chunk)
    q_dec = q * jnp.exp((pos + 1)[:, None] * log_g[None, :])[..., None]
    o_cross = jnp.einsum('bnihd,nbhde->bnihe', q_dec, s_prev)
    return (o_inner + o_cross).reshape(b, L, H, dv), s_final


def complex_affine_combine(e1, e2):
    a1r, a1i, b1r, b1i = e1
    a2r, a2i, b2r, b2i = e2
    return (a2r * a1r - a2i * a1i, a2r * a1i + a2i * a1r,
            a2r * b1r - a2i * b1i + b2r, a2r * b1i + a2i * b1r + b2i)


def s5_scan(u, a_re, a_im, log_dt, b_re, b_im, c_re, c_im, x0_re, x0_im):
    f32 = jnp.float32
    a_re = jnp.minimum(a_re.astype(f32), -1e-4)
    a_im = a_im.astype(f32)
    dt = jnp.exp(log_dt.astype(f32))[:, None]
    mag = jnp.exp(dt * a_re)
    ab_re = mag * jnp.cos(dt * a_im)
    ab_im = mag * jnp.sin(dt * a_im)
    den = a_re * a_re + a_im * a_im
    f_re = ((ab_re - 1.0) * a_re + ab_im * a_im) / den
    f_im = (ab_im * a_re - (ab_re - 1.0) * a_im) / den
    b_re = b_re.astype(f32)
    b_im = b_im.astype(f32)
    bb_re = f_re[..., None] * b_re - f_im[..., None] * b_im
    bb_im = f_re[..., None] * b_im + f_im[..., None] * b_re
    uf = u.astype(f32)
    bu_re = jnp.einsum('blgp,gnp->blgn', uf, bb_re)
    bu_im = jnp.einsum('blgp,gnp->blgn', uf, bb_im)
    x0_re = x0_re.astype(f32)
    x0_im = x0_im.astype(f32)
    bu_re = bu_re.at[:, 0].add(ab_re * x0_re - ab_im * x0_im)
    bu_im = bu_im.at[:, 0].add(ab_re * x0_im + ab_im * x0_re)
    a_r = jnp.broadcast_to(ab_re, bu_re.shape)
    a_i = jnp.broadcast_to(ab_im, bu_im.shape)
    _, _, x_re, x_im = lax.associative_scan(complex_affine_combine, (a_r, a_i, bu_re, bu_im), axis=1)
    y = (jnp.einsum('blgn,gpn->blgp', x_re, c_re.astype(f32))
         - jnp.einsum('blgn,gpn->blgp', x_im, c_im.astype(f32)))
    return y, x_re[:, -1], x_im[:, -1]


def mixer_ab(h, p, ret0, ssm0_re, ssm0_im, use_pos):
    b, L, _ = h.shape
    qk = RET_HEADS * RET_QK_DIM
    vw = RET_HEADS * RET_V_DIM
    proj = h @ p['w_in']
    q, k, v, g, u = jnp.split(proj, [qk, 2 * qk, 2 * qk + vw, 2 * qk + 2 * vw], axis=-1)
    q = q.reshape(b, L, RET_HEADS, RET_QK_DIM) * (RET_QK_DIM ** -0.5)
    k = k.reshape(b, L, RET_HEADS, RET_QK_DIM)
    v = v.reshape(b, L, RET_HEADS, RET_V_DIM)
    if use_pos:
        q, k = axial_rope(q), axial_rope(k)
    if ret0 is None:
        ret0 = jnp.zeros((b, 2, RET_HEADS, RET_QK_DIM, RET_V_DIM), jnp.float32)
        ssm0_re = jnp.zeros((b, 2, SSM_GROUPS, SSM_STATE), jnp.float32)
        ssm0_im = jnp.zeros((b, 2, SSM_GROUPS, SSM_STATE), jnp.float32)
    flip = lambda t: jnp.flip(t, axis=1)
    log_g = jax.nn.log_sigmoid(p['ret_logit'].astype(jnp.float32))
    o_f, s_f = retention_chunkwise(q, k, v, log_g[0], ret0[:, 0])
    o_b, s_b = retention_chunkwise(flip(q), flip(k), flip(v), log_g[1], ret0[:, 1])
    o = o_f + flip(o_b)
    o = o * lax.rsqrt(jnp.mean(o * o, axis=-1, keepdims=True) + NORM_EPS)
    o = o.reshape(b, L, vw).astype(h.dtype) * jax.nn.silu(g)
    ug = u.reshape(b, L, SSM_GROUPS, SSM_GROUP)
    y_f, xf_re, xf_im = s5_scan(ug, p['a_re'][0], p['a_im'][0], p['log_dt'][0], p['b_re'][0], p['b_im'][0],
                                p['c_re'][0], p['c_im'][0], ssm0_re[:, 0], ssm0_im[:, 0])
    y_b, xb_re, xb_im = s5_scan(flip(ug), p['a_re'][1], p['a_im'][1], p['log_dt'][1], p['b_re'][1], p['b_im'][1],
                                p['c_re'][1], p['c_im'][1], ssm0_re[:, 1], ssm0_im[:, 1])
    y = (y_f + flip(y_b)).reshape(b, L, SSM_WIDTH) + p['d'] * u
    z = jax.nn.gelu(y)
    z = (z * jax.nn.sigmoid(z @ p['w_glu'])).astype(h.dtype)
    out = jnp.concatenate([o, z], axis=-1) @ p['w_out']
    states = (jnp.stack([s_f, s_b], axis=1), jnp.stack([xf_re, xb_re], axis=1), jnp.stack([xf_im, xb_im], axis=1))
    return out, states


def blocked_attention(q, k, v):
    b, Lq, nh, d = q.shape
    nq = Lq // ATTN_Q_BLOCK
    qb = q.reshape(b, nq, ATTN_Q_BLOCK, nh, d).swapaxes(0, 1)
    scale = d ** -0.5

    def one(qi):
        s = jnp.einsum('bqhd,bkhd->bhqk', qi, k).astype(jnp.float32) * scale
        pr = jax.nn.softmax(s, axis=-1).astype(v.dtype)
        return jnp.einsum('bhqk,bkhe->bqhe', pr, v)

    o = lax.map(one, qb)
    return o.swapaxes(0, 1).reshape(b, Lq, nh, v.shape[-1])


def mla_keys(ckv, kr, w_ukv):
    b, L, _ = ckv.shape
    kv = (ckv @ w_ukv).reshape(b, L, MLA_HEADS, MLA_NOPE + MLA_V)
    k_nope, v = kv[..., :MLA_NOPE], kv[..., MLA_NOPE:]
    k_rope = jnp.broadcast_to(kr[:, :, None, :], (b, L, MLA_HEADS, MLA_ROPE)).astype(k_nope.dtype)
    return jnp.concatenate([k_nope, k_rope], axis=-1), v


def mla(h, p, ctx_ckv, ctx_kr, use_pos):
    b, L, _ = h.shape
    cq, ckv, kr = jnp.split(h @ p['w_in'], [MLA_Q_RANK, MLA_Q_RANK + MLA_KV_RANK], axis=-1)
    cq = rmsnorm(cq, p['q_g'])
    ckv = rmsnorm(ckv, p['kv_g'])
    q = (cq @ p['w_uq']).reshape(b, L, MLA_HEADS, MLA_NOPE + MLA_ROPE)
    kr_pos = kr
    if use_pos:
        q = jnp.concatenate([q[..., :MLA_NOPE], axial_rope(q[..., MLA_NOPE:])], axis=-1)
        kr_pos = axial_rope(kr[:, :, None, :])[:, :, 0, :]
    k, v = mla_keys(ckv, kr_pos, p['w_ukv'])
    if ctx_ckv is not None:
        kc, vc = mla_keys(ctx_ckv, ctx_kr, p['w_ukv'])
        k = jnp.concatenate([kc, k], axis=1)
        v = jnp.concatenate([vc, v], axis=1)
    o = blocked_attention(q, k, v)
    out = o.reshape(b, L, MLA_HEADS * MLA_V) @ p['w_out']
    return out, (ckv, kr)


def moe(x, router_w, router_b, w_in, b_in, w_out, b_out):
    shp = x.shape
    xt = x.reshape(-1, shp[-1])
    n = xt.shape[0]
    nk = n * TOP_K
    logits = (xt @ router_w + router_b).astype(jnp.float32)
    top_v, top_i = lax.top_k(logits, TOP_K)
    gates = jax.nn.softmax(top_v, axis=-1)
    flat_e = top_i.reshape(-1)
    flat_tok = jnp.arange(nk, dtype=jnp.int32) // TOP_K
    flat_g = gates.reshape(-1)
    order = jnp.argsort(flat_e)
    se, stok, sg = flat_e[order], flat_tok[order], flat_g[order]
    counts = jnp.bincount(flat_e, length=N_EXPERTS)
    starts = jnp.cumsum(counts) - counts
    padded = ((counts + MOE_BLOCK - 1) // MOE_BLOCK) * MOE_BLOCK
    pad_ends = jnp.cumsum(padded)
    pad_starts = pad_ends - padded
    dest = pad_starts[se] + (jnp.arange(nk) - starts[se])
    P = (-(-nk // MOE_BLOCK) + N_EXPERTS) * MOE_BLOCK
    nblk = P // MOE_BLOCK
    buf_tok = jnp.full((P,), n, jnp.int32).at[dest].set(stok)
    buf_g = jnp.zeros((P,), jnp.float32).at[dest].set(sg)
    blk_e = jnp.minimum(jnp.searchsorted(pad_ends, jnp.arange(nblk) * MOE_BLOCK, side='right'), N_EXPERTS - 1)
    xpad = jnp.concatenate([xt, jnp.zeros((1, xt.shape[1]), xt.dtype)], axis=0)

    def run(args):
        e, toks, g = args
        hb = xpad[toks] @ w_in[e] + b_in[e]
        glu, lin = hb[:, :D_EXPERT], hb[:, D_EXPERT:]
        glu = jnp.minimum(glu, SWIGLU_LIMIT)
        lin = jnp.clip(lin, -SWIGLU_LIMIT, SWIGLU_LIMIT)
        act = glu * jax.nn.sigmoid(SWIGLU_ALPHA * glu) * (lin + 1)
        y = act @ w_out[e] + b_out[e]
        return (y * g[:, None]).astype(x.dtype)

    ys = lax.map(run, (blk_e, buf_tok.reshape(nblk, MOE_BLOCK), buf_g.reshape(nblk, MOE_BLOCK)))
    out = jax.ops.segment_sum(ys.reshape(P, -1), buf_tok, num_segments=n + 1)[:n]
    return out.reshape(shp)


def block(x, mod, g1, g2, moe_p, mix_fn):
    shift1, scale1, gate1, shift2, scale2, gate2 = [mod[:, j][:, None, :] for j in range(6)]
    out, aux = mix_fn(modulate(rmsnorm(x, g1), shift1, scale1))
    x = x + gate1 * out
    x = x + gate2 * moe(modulate(rmsnorm(x, g2), shift2, scale2), *moe_p)
    return x, aux


def setup_inputs(seed: int = 0) -> dict:
    key = jax.random.key(seed)
    ks = iter(jax.random.split(key, 64))
    f32 = jnp.float32
    nrm = lambda shape, s: jax.random.normal(next(ks), shape, f32) * s
    gamma = 1.0 - 2.0 ** (-5.0 - jnp.arange(RET_HEADS, dtype=f32))
    ret_logit0 = jnp.log(gamma) - jnp.log1p(-gamma)
    n_idx = jnp.arange(SSM_STATE, dtype=f32)
    ssm_shape = (N_AB_LAYERS, 2, SSM_GROUPS, SSM_STATE)
    return {
        'x_prompt': nrm((BATCH, SEQ, D_MODEL), 1.0),
        'x_sample': nrm((DEC_BATCH, DEC_SEQ, D_MODEL), 1.0),
        'c': nrm((DEC_BATCH, D_MODEL), 1.0),
        'state_ret': nrm((DEC_BATCH, N_AB_LAYERS, 2, RET_HEADS, RET_QK_DIM, RET_V_DIM), 1.0),
        'state_ssm_re': nrm((DEC_BATCH,) + ssm_shape[1:2] and (DEC_BATCH, N_AB_LAYERS, 2, SSM_GROUPS, SSM_STATE), 1.0),
        'state_ssm_im': nrm((DEC_BATCH, N_AB_LAYERS, 2, SSM_GROUPS, SSM_STATE), 1.0),
        'cache_mla_ckv': nrm((DEC_BATCH, N_C_LAYERS, PAST_LEN, MLA_KV_RANK), 1.0),
        'cache_mla_krope': nrm((DEC_BATCH, N_C_LAYERS, PAST_LEN, MLA_ROPE), 1.0),
        'c_ctx': nrm((D_MODEL,), 1.0),
        'w_ada': nrm((DEPTH, D_MODEL, 6 * D_MODEL), 0.5 * D_MODEL ** -0.5),
        'b_ada': nrm((DEPTH, 6 * D_MODEL), 0.02),
        'norm1_g': 1.0 + nrm((DEPTH, D_MODEL), 0.05),
        'norm2_g': 1.0 + nrm((DEPTH, D_MODEL), 0.05),
        'final_norm_g': 1.0 + nrm((D_MODEL,), 0.05),
        'w_in_ab': nrm((N_AB_LAYERS, D_MODEL, AB_IN_WIDTH), D_MODEL ** -0.5),
        'w_out_ab': nrm((N_AB_LAYERS, MIX_WIDTH, D_MODEL), MIX_WIDTH ** -0.5),
        'ret_decay_logit': ret_logit0 + nrm((N_AB_LAYERS, 2, RET_HEADS), 0.05),
        'ssm_a_re': -0.5 + nrm(ssm_shape, 0.01),
        'ssm_a_im': jnp.pi * n_idx + nrm(ssm_shape, 0.01),
        'ssm_log_dt': jax.random.uniform(next(ks), (N_AB_LAYERS, 2, SSM_GROUPS), f32,
                                         minval=math.log(1e-3), maxval=math.log(1e-1)),
        'ssm_b_re': nrm((N_AB_LAYERS, 2, SSM_GROUPS, SSM_STATE, SSM_GROUP), (2 * SSM_GROUP) ** -0.5),
        'ssm_b_im': nrm((N_AB_LAYERS, 2, SSM_GROUPS, SSM_STATE, SSM_GROUP), (2 * SSM_GROUP) ** -0.5),
        'ssm_c_re': nrm((N_AB_LAYERS, 2, SSM_GROUPS, SSM_GROUP, SSM_STATE), (2 * SSM_STATE) ** -0.5),
        'ssm_c_im': nrm((N_AB_LAYERS, 2, SSM_GROUPS, SSM_GROUP, SSM_STATE), (2 * SSM_STATE) ** -0.5),
        'ssm_d': nrm((N_AB_LAYERS, SSM_WIDTH), 1.0),
        'ssm_w_glu': nrm((N_AB_LAYERS, SSM_WIDTH, SSM_WIDTH), SSM_WIDTH ** -0.5),
        'w_in_c': nrm((N_C_LAYERS, D_MODEL, C_IN_WIDTH), D_MODEL ** -0.5),
        'mla_q_norm_g': 1.0 + nrm((N_C_LAYERS, MLA_Q_RANK), 0.05),
        'mla_kv_norm_g': 1.0 + nrm((N_C_LAYERS, MLA_KV_RANK), 0.05),
        'mla_w_uq': nrm((N_C_LAYERS, MLA_Q_RANK, MLA_HEADS * (MLA_NOPE + MLA_ROPE)), MLA_Q_RANK ** -0.5),
        'mla_w_ukv': nrm((N_C_LAYERS, MLA_KV_RANK, MLA_HEADS * (MLA_NOPE + MLA_V)), MLA_KV_RANK ** -0.5),
        'w_out_c': nrm((N_C_LAYERS, MIX_WIDTH, D_MODEL), MIX_WIDTH ** -0.5),
        'router_w': nrm((DEPTH, D_MODEL, N_EXPERTS), D_MODEL ** -0.5),
        'router_b': nrm((DEPTH, N_EXPERTS), 0.01),
        'moe_w_in': nrm((DEPTH, N_EXPERTS, D_MODEL, 2 * D_EXPERT), D_MODEL ** -0.5),
        'moe_b_in': nrm((DEPTH, N_EXPERTS, 2 * D_EXPERT), 0.01),
        'moe_w_out': nrm((DEPTH, N_EXPERTS, D_EXPERT, D_MODEL), D_EXPERT ** -0.5),
        'moe_b_out': nrm((DEPTH, N_EXPERTS, D_MODEL), 0.01),
    }


def reference(x_prompt, x_sample, c, state_ret, state_ssm_re, state_ssm_im, cache_mla_ckv, cache_mla_krope,
              c_ctx, w_ada, b_ada, norm1_g, norm2_g, final_norm_g,
              w_in_ab, w_out_ab, ret_decay_logit, ssm_a_re, ssm_a_im, ssm_log_dt, ssm_b_re, ssm_b_im,
              ssm_c_re, ssm_c_im, ssm_d, ssm_w_glu,
              w_in_c, mla_q_norm_g, mla_kv_norm_g, mla_w_uq, mla_w_ukv, w_out_c,
              router_w, router_b, moe_w_in, moe_b_in, moe_w_out, moe_b_out):
    xc, xs = x_prompt, x_sample
    silu_ctx = jax.nn.silu(c_ctx)[None, :]
    silu_c = jax.nn.silu(c)
    ret_l, ssm_re_l, ssm_im_l, ckv_l, kr_l = [], [], [], [], []
    for l in range(DEPTH):
        mod_ctx = (silu_ctx @ w_ada[l] + b_ada[l]).reshape(1, 6, D_MODEL)
        mod_lat = (silu_c @ w_ada[l] + b_ada[l]).reshape(c.shape[0], 6, D_MODEL)
        moe_p = (router_w[l], router_b[l], moe_w_in[l], moe_b_in[l], moe_w_out[l], moe_b_out[l])
        i = l // 2
        if l % 2 == 0:
            p = {'w_in': w_in_ab[i], 'w_out': w_out_ab[i], 'ret_logit': ret_decay_logit[i],
                 'a_re': ssm_a_re[i], 'a_im': ssm_a_im[i], 'log_dt': ssm_log_dt[i],
                 'b_re': ssm_b_re[i], 'b_im': ssm_b_im[i], 'c_re': ssm_c_re[i], 'c_im': ssm_c_im[i],
                 'd': ssm_d[i], 'w_glu': ssm_w_glu[i]}
            xc, (s_ret, s_re, s_im) = block(xc, mod_ctx, norm1_g[l], norm2_g[l], moe_p,
                                            lambda h: mixer_ab(h, p, None, None, None, False))
            xs, _ = block(xs, mod_lat, norm1_g[l], norm2_g[l], moe_p,
                          lambda h: mixer_ab(h, p, state_ret[:, i], state_ssm_re[:, i], state_ssm_im[:, i], True))
            ret_l.append(s_ret)
            ssm_re_l.append(s_re)
            ssm_im_l.append(s_im)
        else:
            p = {'w_in': w_in_c[i], 'w_out': w_out_c[i], 'q_g': mla_q_norm_g[i], 'kv_g': mla_kv_norm_g[i],
                 'w_uq': mla_w_uq[i], 'w_ukv': mla_w_ukv[i]}
            xc, (ckv, kr) = block(xc, mod_ctx, norm1_g[l], norm2_g[l], moe_p,
                                  lambda h: mla(h, p, None, None, False))
            xs, _ = block(xs, mod_lat, norm1_g[l], norm2_g[l], moe_p,
                          lambda h: mla(h, p, cache_mla_ckv[:, i], cache_mla_krope[:, i], True))
            ckv_l.append(ckv)
            kr_l.append(kr)
    y_prompt = rmsnorm(xc, final_norm_g)
    y_sample = rmsnorm(xs, final_norm_g)
    new_state_ret = jnp.stack(ret_l, axis=1)
    new_state_ssm_re = jnp.stack(ssm_re_l, axis=1)
    new_state_ssm_im = jnp.stack(ssm_im_l, axis=1)
    new_cache_mla_ckv = jnp.stack(ckv_l, axis=1)
    new_cache_mla_krope = jnp.stack(kr_l, axis=1)
    return (y_prompt, y_sample, new_state_ret, new_state_ssm_re, new_state_ssm_im, new_cache_mla_ckv, new_cache_mla_krope)
```

```python
import functools
import math

import jax
import jax.numpy as jnp
from jax import lax
from jax.experimental import pallas as pl
from jax.experimental.pallas import tpu as pltpu

F32 = jnp.float32
BF16 = jnp.bfloat16
HIGHEST = lax.Precision.HIGHEST

D_MODEL = 1024
BATCH = 32
SEQ = 256
DEPTH = 4
DEC_BATCH = 8
DEC_SEQ = 1024
PAST_LEN = 512
GRID_W = 64
RET_HEADS = 4
RET_V_DIM = 128
RET_QK_DIM = 64
RET_CHUNK = 128
SSM_WIDTH = 512
SSM_GROUP = 16
SSM_GROUPS = 32
SSM_STATE = 64
SSM_CHUNK = 16
MLA_HEADS = 8
MLA_NOPE = 64
MLA_ROPE = 32
MLA_V = 128
MLA_Q_RANK = 256
MLA_KV_RANK = 128
N_EXPERTS = 32
TOP_K = 4
D_EXPERT = 1024
SWIGLU_LIMIT = 7.0
SWIGLU_ALPHA = 1.702
ROPE_BASE = 10000.0
NORM_EPS = 1e-6

NP = BATCH * SEQ
NS = DEC_BATCH * DEC_SEQ
N_TOK = NP + NS
N_SEG = 1 + DEC_BATCH
LANES = 128

TM = 512
TMC = 256
TME = 512
N_ASSIGN = N_TOK * TOP_K
N_EBLK = N_ASSIGN // TME + N_EXPERTS
P_ROWS = N_EBLK * TME
VMEM_LIMIT = 56 * 1024 * 1024


def _params(sem, vmem=VMEM_LIMIT):
    return pltpu.CompilerParams(dimension_semantics=sem, vmem_limit_bytes=vmem)


def _bdot(a, b):
    return jnp.dot(a.astype(BF16), b.astype(BF16), preferred_element_type=F32)


def _split(x):
    hi = x.astype(BF16)
    lo = (x - hi.astype(F32)).astype(BF16)
    return hi, lo


def _dot3(a, b):
    ah, al = _split(a)
    bh, bl = _split(b)
    d = functools.partial(jnp.dot, preferred_element_type=F32)
    return d(ah, bh) + d(al, bh) + d(ah, bl)


def _rms(x, g):
    return x * lax.rsqrt(jnp.mean(x * x, axis=-1, keepdims=True) + NORM_EPS) * g


SLAB = D_MODEL // LANES


def _store_slabs(ref, row0, val):
    m = val.shape[0]
    for s in range(SLAB):
        ref[pl.ds(row0 * SLAB + s, m, stride=SLAB), :] = val[:, s * LANES:(s + 1) * LANES]


def _load_slabs(ref, row0, m):
    return jnp.concatenate([ref[pl.ds(row0 * SLAB + s, m, stride=SLAB), :] for s in range(SLAB)], axis=-1)


def _slab_copy(src, dst, sem, src_row, dst_row):
    return pltpu.make_async_copy(src.at[pl.ds(pl.multiple_of(src_row * SLAB, SLAB), SLAB)],
                                 dst.at[pl.ds(pl.multiple_of(dst_row * SLAB, SLAB), SLAB)], sem)


def _seg(i, tm):
    return jnp.where(i < NP // tm, 0, 1 + (i * tm - NP) // DEC_SEQ)


def _mod_kernel(c_ref, w_ref, b_ref, o_ref):
    c = c_ref[...]
    s = c * jax.nn.sigmoid(c)
    o_ref[0] = _dot3(s, w_ref[0]) + b_ref[0]


def _modulation(c_ctx, c, w_ada, b_ada):
    rows = 16
    cvec = jnp.zeros((rows, D_MODEL), F32).at[0].set(c_ctx).at[1:N_SEG].set(c)
    nb = 1536
    out = pl.pallas_call(
        _mod_kernel,
        grid=(DEPTH, 6 * D_MODEL // nb),
        in_specs=[pl.BlockSpec((rows, D_MODEL), lambda l, j: (0, 0)),
                  pl.BlockSpec((1, D_MODEL, nb), lambda l, j: (l, 0, j)),
                  pl.BlockSpec((1, 1, nb), lambda l, j: (l, 0, j))],
        out_specs=pl.BlockSpec((1, rows, nb), lambda l, j: (l, 0, j)),
        out_shape=jax.ShapeDtypeStruct((DEPTH, rows, 6 * D_MODEL), F32),
        compiler_params=_params(("arbitrary", "arbitrary")),
        name="adaln_mod",
    )(cvec, w_ada, b_ada.reshape(DEPTH, 1, 6 * D_MODEL))
    return out[:, :N_SEG].reshape(DEPTH, N_SEG, 6, D_MODEL)


def _pre_ab_kernel(x_ref, g_ref, mod_ref, w_ref, qkvg_ref, u_ref):
    m = mod_ref[0]
    h = _rms(x_ref[...], g_ref[...]) * (1.0 + m[1:2]) + m[0:1]
    p = _bdot(h, w_ref[...])
    qkvg_ref[...] = p[:, :1536]
    u_ref[...] = p[:, 1536:]


def _pre_ab(x, g1, mod, w_in):
    return pl.pallas_call(
        _pre_ab_kernel,
        grid=(N_TOK // TM,),
        in_specs=[pl.BlockSpec((TM, D_MODEL), lambda i: (i, 0)),
                  pl.BlockSpec((1, D_MODEL), lambda i: (0, 0)),
                  pl.BlockSpec((1, 6, D_MODEL), lambda i: (_seg(i, TM), 0, 0)),
                  pl.BlockSpec((D_MODEL, 2048), lambda i: (0, 0))],
        out_specs=[pl.BlockSpec((TM, 1536), lambda i: (i, 0)),
                   pl.BlockSpec((TM, 512), lambda i: (i, 0))],
        out_shape=[jax.ShapeDtypeStruct((N_TOK, 1536), F32),
                   jax.ShapeDtypeStruct((N_TOK, 512), F32)],
        compiler_params=_params(("arbitrary",)),
        name="pre_ab",
    )(x, g1.reshape(1, D_MODEL), mod, w_in)


def _rope_lanes(x, cos, sins, first, shift):
    w = x.shape[-1]
    partner = jnp.where(first, pltpu.roll(x, w - shift, 1), pltpu.roll(x, shift, 1))
    return x * cos + partner * sins


def _pre_c_kernel(x_ref, g_ref, mod_ref, w_ref, qg_ref, kvg_ref, wuq_ref, cos_ref, sin_ref,
                  q_ref, ckv_ref, kr_ref, krp_ref):
    m = mod_ref[0]
    h = _rms(x_ref[...], g_ref[...]) * (1.0 + m[1:2]) + m[0:1]
    p = _bdot(h, w_ref[...])
    cq = _rms(p[:, :MLA_Q_RANK], qg_ref[...])
    ckv_ref[...] = _rms(p[:, MLA_Q_RANK:MLA_Q_RANK + MLA_KV_RANK], kvg_ref[...])
    kr = p[:, MLA_Q_RANK + MLA_KV_RANK:]
    kr_ref[...] = kr
    cos = cos_ref[...]
    sins = sin_ref[...]
    lane = lax.broadcasted_iota(jnp.int32, cos.shape, 1)
    first = (lane % 16) < 8
    krp_ref[...] = _rope_lanes(kr, cos, sins, first, 8)
    q = _bdot(cq, wuq_ref[...])
    for hh in range(MLA_HEADS):
        sl = slice(hh * LANES, (hh + 1) * LANES)
        q_ref[:, sl] = _rope_lanes(q[:, sl], cos, sins, first, 8).astype(BF16)


def _pre_c(x, g1, mod, w_in_pad, q_g, kv_g, w_uq_pad, cos_t, sin_t):
    def tab(i):
        return (jnp.where(i < NP // TM, 0, 1 + ((i * TM - NP) % DEC_SEQ) // TM), 0)
    return pl.pallas_call(
        _pre_c_kernel,
        grid=(N_TOK // TM,),
        in_specs=[pl.BlockSpec((TM, D_MODEL), lambda i: (i, 0)),
                  pl.BlockSpec((1, D_MODEL), lambda i: (0, 0)),
                  pl.BlockSpec((1, 6, D_MODEL), lambda i: (_seg(i, TM), 0, 0)),
                  pl.BlockSpec((D_MODEL, 512), lambda i: (0, 0)),
                  pl.BlockSpec((1, MLA_Q_RANK), lambda i: (0, 0)),
                  pl.BlockSpec((1, MLA_KV_RANK), lambda i: (0, 0)),
                  pl.BlockSpec((MLA_Q_RANK, MLA_HEADS * LANES), lambda i: (0, 0)),
                  pl.BlockSpec((TM, LANES), tab),
                  pl.BlockSpec((TM, LANES), tab)],
        out_specs=[pl.BlockSpec((TM, MLA_HEADS * LANES), lambda i: (i, 0)),
                   pl.BlockSpec((TM, MLA_KV_RANK), lambda i: (i, 0)),
                   pl.BlockSpec((TM, LANES), lambda i: (i, 0)),
                   pl.BlockSpec((TM, LANES), lambda i: (i, 0))],
        out_shape=[jax.ShapeDtypeStruct((N_TOK, MLA_HEADS * LANES), BF16),
                   jax.ShapeDtypeStruct((N_TOK, MLA_KV_RANK), F32),
                   jax.ShapeDtypeStruct((N_TOK, LANES), F32),
                   jax.ShapeDtypeStruct((N_TOK, LANES), F32)],
        compiler_params=_params(("arbitrary",)),
        name="pre_c",
    )(x, g1.reshape(1, D_MODEL), mod, w_in_pad, q_g.reshape(1, -1), kv_g.reshape(1, -1), w_uq_pad,
      cos_t, sin_t)


def _s5_tables(a_re, a_im, log_dt, b_re, b_im, c_re, c_im):
    C = SSM_CHUNK
    a_re = jnp.minimum(a_re.astype(F32), -1e-4)
    a_im = a_im.astype(F32)
    dt = jnp.exp(log_dt.astype(F32))[..., None]
    mag = jnp.exp(dt * a_re)
    ab_re = mag * jnp.cos(dt * a_im)
    ab_im = mag * jnp.sin(dt * a_im)
    den = a_re * a_re + a_im * a_im
    f_re = ((ab_re - 1.0) * a_re + ab_im * a_im) / den
    f_im = (ab_im * a_re - (ab_re - 1.0) * a_im) / den
    b_re = b_re.astype(F32)
    b_im = b_im.astype(F32)
    bb_re = f_re[..., None] * b_re - f_im[..., None] * b_im
    bb_im = f_re[..., None] * b_im + f_im[..., None] * b_re
    ks = jnp.arange(C + 1, dtype=F32)[:, None, None, None]
    pmag = jnp.exp(ks * (dt * a_re)[None])
    p_re = pmag * jnp.cos(ks * (dt * a_im)[None])
    p_im = pmag * jnp.sin(ks * (dt * a_im)[None])
    c_re = c_re.astype(F32)[None]
    c_im = c_im.astype(F32)[None]
    cl_re = c_re * p_re[:, :, :, None, :] - c_im * p_im[:, :, :, None, :]
    cl_im = c_re * p_im[:, :, :, None, :] + c_im * p_re[:, :, :, None, :]
    kmat = (jnp.einsum('kdgpn,dgnq->kdgpq', cl_re[:C], bb_re, precision=HIGHEST)
            - jnp.einsum('kdgpn,dgnq->kdgpq', cl_im[:C], bb_im, precision=HIGHEST))
    s_idx = jnp.arange(C)[:, None]
    j_idx = jnp.arange(C)[None, :]

    def toeplitz(k, diff):
        t = k[jnp.clip(diff, 0, C - 1)]
        t = jnp.where((diff >= 0)[:, :, None, None, None], t, 0.0)
        return t.transpose(2, 0, 4, 1, 3).reshape(SSM_GROUPS, C * SSM_GROUP, C * SSM_GROUP)

    m_loc = toeplitz(kmat[:, 0], j_idx - s_idx) + toeplitz(kmat[:, 1], s_idx - j_idx)

    def state_in(d, expo):
        pr = p_re[expo, d][..., None]
        pi = p_im[expo, d][..., None]
        er = pr * bb_re[d][None] - pi * bb_im[d][None]
        ei = pr * bb_im[d][None] + pi * bb_re[d][None]
        to = lambda e: e.transpose(1, 0, 3, 2).reshape(SSM_GROUPS, C * SSM_GROUP, SSM_STATE)
        return to(er), to(ei)

    ef_re, ef_im = state_in(0, C - 1 - jnp.arange(C))
    eb_re, eb_im = state_in(1, jnp.arange(C))
    w1 = jnp.concatenate([m_loc, ef_re, eb_re, ef_im, eb_im], axis=-1)

    def state_out(d, expo):
        to = lambda e: e.transpose(1, 3, 0, 2).reshape(SSM_GROUPS, SSM_STATE, C * SSM_GROUP)
        return to(cl_re[expo, d]), to(-cl_im[expo, d])

    ff_re, ff_im = state_out(0, 1 + jnp.arange(C))
    fb_re, fb_im = state_out(1, C - jnp.arange(C))
    w2 = jnp.concatenate([ff_re, fb_re, ff_im, fb_im], axis=1)
    a16 = jnp.stack([jnp.concatenate([p_re[C, 0], p_re[C, 1]], axis=-1),
                     jnp.concatenate([p_im[C, 0], p_im[C, 1]], axis=-1)], axis=1)
    return w1.astype(BF16), w2.astype(BF16), a16


def _s5_kernel(nb, nc, u_ref, w1_ref, w2_ref, a_ref, x0_ref, y_ref, xf_ref, r1_ref, s_ref, l2_ref):
    r1_ref[...] = _bdot(u_ref[0], w1_ref[0])
    a = a_ref[0]
    a_re = a[0:1]
    a_im = a[1:2]
    fwd = lax.broadcasted_iota(jnp.int32, (nb, LANES), 1) < SSM_STATE
    s_re = x0_ref[0, 0]
    s_im = x0_ref[0, 1]
    s_ref[0, 0] = s_re
    s_ref[1, 0] = s_im
    for k in range(nc):
        rf = slice(k * nb, (k + 1) * nb)
        rb = slice((nc - 1 - k) * nb, (nc - k) * nb)
        l_re = jnp.where(fwd, r1_ref[rf, 256:384], r1_ref[rb, 256:384])
        l_im = jnp.where(fwd, r1_ref[rf, 384:512], r1_ref[rb, 384:512])
        n_re = a_re * s_re - a_im * s_im + l_re
        n_im = a_re * s_im + a_im * s_re + l_im
        s_re, s_im = n_re, n_im
        s_ref[0, k + 1] = s_re
        s_ref[1, k + 1] = s_im
    xf_ref[0, 0] = s_re
    xf_ref[0, 1] = s_im
    for c in range(nc):
        rows = slice(c * nb, (c + 1) * nb)
        l2_ref[rows, 0:128] = jnp.where(fwd, s_ref[0, c], s_ref[0, nc - 1 - c])
        l2_ref[rows, 128:256] = jnp.where(fwd, s_ref[1, c], s_ref[1, nc - 1 - c])
    y_ref[0] = r1_ref[:, 0:256] + _bdot(l2_ref[...], w2_ref[0])


def _s5(u_t, w1, w2, a16, x0, nb, nc):
    r = nb * nc
    return pl.pallas_call(
        functools.partial(_s5_kernel, nb, nc),
        grid=(SSM_GROUPS,),
        in_specs=[pl.BlockSpec((1, r, 256), lambda g: (g, 0, 0)),
                  pl.BlockSpec((1, 256, 512), lambda g: (g, 0, 0)),
                  pl.BlockSpec((1, 256, 256), lambda g: (g, 0, 0)),
                  pl.BlockSpec((1, 2, LANES), lambda g: (g, 0, 0)),
                  pl.BlockSpec((1, 2, nb, LANES), lambda g: (g, 0, 0, 0))],
        out_specs=[pl.BlockSpec((1, r, 256), lambda g: (g, 0, 0)),
                   pl.BlockSpec((1, 2, nb, LANES), lambda g: (g, 0, 0, 0))],
        out_shape=[jax.ShapeDtypeStruct((SSM_GROUPS, r, 256), F32),
                   jax.ShapeDtypeStruct((SSM_GROUPS, 2, nb, LANES), F32)],
        scratch_shapes=[pltpu.VMEM((r, 512), F32),
                        pltpu.VMEM((2, nc + 1, nb, LANES), F32),
                        pltpu.VMEM((r, 256), F32)],
        compiler_params=_params(("arbitrary",)),
        name="s5_chunked",
    )(u_t, w1, w2, a16, x0)


def _to_chunks(u, nb, seq):
    nc = seq // SSM_CHUNK
    return (u.reshape(nb, nc, SSM_CHUNK, SSM_GROUPS, SSM_GROUP).transpose(3, 1, 0, 2, 4)
            .reshape(SSM_GROUPS, nc * nb, SSM_CHUNK * SSM_GROUP))


def _from_chunks(y, nb, seq):
    nc = seq // SSM_CHUNK
    return (y.reshape(SSM_GROUPS, nc, nb, SSM_CHUNK, SSM_GROUP).transpose(2, 1, 3, 0, 4)
            .reshape(nb * seq, SSM_WIDTH))


def _ret_tables(logit):
    C = RET_CHUNK
    lg = jax.nn.log_sigmoid(logit.astype(F32))
    pos = jnp.arange(C, dtype=F32)
    diff = pos[:, None] - pos[None, :]
    dsum = (jnp.where(diff >= 0, jnp.exp(jnp.maximum(diff, 0.0)[None] * lg[0][:, None, None]), 0.0)
            + jnp.where(diff <= 0, jnp.exp(jnp.maximum(-diff, 0.0)[None] * lg[1][:, None, None]), 0.0))
    kdf = jnp.exp((C - 1 - pos)[:, None] * lg[0][None, :])
    kdb = jnp.exp(pos[:, None] * lg[1][None, :])
    qdf = jnp.exp((pos + 1)[:, None] * lg[0][None, :])
    qdb = jnp.exp((C - pos)[:, None] * lg[1][None, :])
    dec = jnp.stack([jnp.repeat(t, RET_QK_DIM, axis=1) for t in (kdf, kdb, qdf, qdb)])
    cd = jnp.exp(C * lg)
    return dsum, dec, cd


def _ab_core_kernel(seq, use_pos, with_state, *refs):
    it = iter(refs)
    qkvg_ref, u_ref, ycv_ref, dsum_ref, dec_ref, cd_ref, d_ref, wglu_ref = [next(it) for _ in range(8)]
    if use_pos:
        cos_ref, sin_ref = next(it), next(it)
    s0_ref = next(it) if use_pos else None
    next(it)
    mix_ref = next(it)
    st_ref = next(it) if with_state else None
    C = RET_CHUNK
    n = seq // C
    q = qkvg_ref[:, 0:256] * (RET_QK_DIM ** -0.5)
    k = qkvg_ref[:, 256:512]
    if use_pos:
        cos = cos_ref[...]
        sins = sin_ref[...]
        first = (lax.broadcasted_iota(jnp.int32, cos.shape, 1) % 32) < 16
        q = _rope_lanes(q, cos, sins, first, 16)
        k = _rope_lanes(k, cos, sins, first, 16)
    kdf, kdb, qdf, qdb = dec_ref[0], dec_ref[1], dec_ref[2], dec_ref[3]
    for h in range(RET_HEADS):
        qs = slice(h * RET_QK_DIM, (h + 1) * RET_QK_DIM)
        vs = slice(512 + h * RET_V_DIM, 512 + (h + 1) * RET_V_DIM)
        gs = slice(1024 + h * RET_V_DIM, 1024 + (h + 1) * RET_V_DIM)
        qc, kc, vc, kvf, kvb = [], [], [], [], []
        for c in range(n):
            rows = slice(c * C, (c + 1) * C)
            qc.append(q[rows, qs])
            kc.append(k[rows, qs])
            vc.append(qkvg_ref[rows, vs])
            kvf.append(_bdot((kc[c] * kdf[:, qs]).T, vc[c]))
            kvb.append(_bdot((kc[c] * kdb[:, qs]).T, vc[c]))
        if use_pos:
            sf = s0_ref[0, 0, h]
            sb = s0_ref[0, 1, h]
        else:
            sf = jnp.zeros((RET_QK_DIM, RET_V_DIM), F32)
            sb = jnp.zeros((RET_QK_DIM, RET_V_DIM), F32)
        prev_f = []
        for c in range(n):
            prev_f.append(sf)
            sf = cd_ref[0, h] * sf + kvf[c]
        next_b = [None] * n
        for c in range(n - 1, -1, -1):
            next_b[c] = sb
            sb = cd_ref[1, h] * sb + kvb[c]
        if with_state:
            st_ref[0, 0, h] = sf
            st_ref[0, 1, h] = sb
        for c in range(n):
            rows = slice(c * C, (c + 1) * C)
            sc = lax.dot_general(qc[c].astype(BF16), kc[c].astype(BF16), (((1,), (1,)), ((), ())),
                                 preferred_element_type=F32) * dsum_ref[h]
            o = (_bdot(sc, vc[c]) + _bdot(qc[c] * qdf[:, qs], prev_f[c])
                 + _bdot(qc[c] * qdb[:, qs], next_b[c]))
            o = o * lax.rsqrt(jnp.mean(o * o, axis=-1, keepdims=True) + NORM_EPS)
            g = qkvg_ref[rows, gs]
            mix_ref[rows, h * RET_V_DIM:(h + 1) * RET_V_DIM] = (o * (g * jax.nn.sigmoid(g))).astype(BF16)
    y = ycv_ref[...] + d_ref[...] * u_ref[...]
    z = 0.5 * y * (1.0 + jnp.tanh(math.sqrt(2.0 / math.pi) * (y + 0.044715 * (y * y * y))))
    z = z * jax.nn.sigmoid(_bdot(z, wglu_ref[...]))
    mix_ref[:, 512:1024] = z.astype(BF16)


def _ab_core(qkvg, u, ycv, dsum, dec, cd, d, w_glu, mix_prev, *, nb, seq, row0, rope=None, s0=None):
    use_pos = rope is not None
    with_state = not use_pos
    blk0 = row0 // seq
    in_specs = [pl.BlockSpec((seq, 1536), lambda b: (blk0 + b, 0)),
                pl.BlockSpec((seq, 512), lambda b: (blk0 + b, 0)),
                pl.BlockSpec((seq, 512), lambda b: (b, 0)),
                pl.BlockSpec((RET_HEADS, RET_CHUNK, RET_CHUNK), lambda b: (0, 0, 0)),
                pl.BlockSpec((4, RET_CHUNK, 256), lambda b: (0, 0, 0)),
                pl.BlockSpec(memory_space=pltpu.SMEM),
                pl.BlockSpec((1, SSM_WIDTH), lambda b: (0, 0)),
                pl.BlockSpec((SSM_WIDTH, SSM_WIDTH), lambda b: (0, 0))]
    args = [qkvg, u, ycv, dsum, dec, cd, d.reshape(1, SSM_WIDTH), w_glu]
    if use_pos:
        in_specs += [pl.BlockSpec((seq, 256), lambda b: (0, 0)),
                     pl.BlockSpec((seq, 256), lambda b: (0, 0)),
                     pl.BlockSpec((1, 2, RET_HEADS, RET_QK_DIM, RET_V_DIM), lambda b: (b, 0, 0, 0, 0))]
        args += [rope[0], rope[1], s0]
    in_specs.append(pl.BlockSpec(memory_space=pl.ANY))
    args.append(mix_prev)
    out_specs = [pl.BlockSpec((seq, D_MODEL), lambda b: (blk0 + b, 0))]
    out_shape = [jax.ShapeDtypeStruct((N_TOK, D_MODEL), BF16)]
    if with_state:
        out_specs.append(pl.BlockSpec((1, 2, RET_HEADS, RET_QK_DIM, RET_V_DIM), lambda b: (b, 0, 0, 0, 0)))
        out_shape.append(jax.ShapeDtypeStruct((nb, 2, RET_HEADS, RET_QK_DIM, RET_V_DIM), F32))
    return pl.pallas_call(
        functools.partial(_ab_core_kernel, seq, use_pos, with_state),
        grid=(nb,),
        in_specs=in_specs,
        out_specs=out_specs,
        out_shape=out_shape,
        input_output_aliases={len(args) - 1: 0},
        compiler_params=_params(("arbitrary",)),
        name="ab_core_pos" if use_pos else "ab_core",
    )(*args)


def _attn_kernel(lk, tq, q_ref, ckv_ref, kr_ref, wukv_ref, _, o_ref, k_sc, v_sc):
    @pl.when(pl.program_id(1) == 0)
    def _():
        kv = _bdot(ckv_ref[0], wukv_ref[...])
        kr = kr_ref[0]
        for h in range(MLA_HEADS):
            sl = slice(h * LANES, (h + 1) * LANES)
            k_sc[:, sl] = (kv[:, sl] + kr).astype(BF16)
        v_sc[...] = kv[:, MLA_HEADS * LANES:].astype(BF16)

    scale = (MLA_NOPE + MLA_ROPE) ** -0.5
    for h in range(MLA_HEADS):
        sl = slice(h * LANES, (h + 1) * LANES)
        s = lax.dot_general(q_ref[:, sl], k_sc[:, sl], (((1,), (1,)), ((), ())),
                            preferred_element_type=F32) * scale
        s = s - jnp.max(s, axis=-1, keepdims=True)
        e = jnp.exp(s)
        p = e / jnp.sum(e, axis=-1, keepdims=True)
        o_ref[:, sl] = _bdot(p, v_sc[:, sl]).astype(BF16)


def _attention(q, ckv_all, kr_all, w_ukv_pad, o_prev, *, nb, seq, lk, row0, tq=256):
    nq = seq // tq
    blk0 = row0 // tq
    return pl.pallas_call(
        functools.partial(_attn_kernel, lk, tq),
        grid=(nb, nq),
        in_specs=[pl.BlockSpec((tq, MLA_HEADS * LANES), lambda b, i: (blk0 + b * nq + i, 0)),
                  pl.BlockSpec((1, lk, MLA_KV_RANK), lambda b, i: (b, 0, 0)),
                  pl.BlockSpec((1, lk, LANES), lambda b, i: (b, 0, 0)),
                  pl.BlockSpec((MLA_KV_RANK, 2 * MLA_HEADS * LANES), lambda b, i: (0, 0)),
                  pl.BlockSpec(memory_space=pl.ANY)],
        out_specs=pl.BlockSpec((tq, MLA_HEADS * MLA_V), lambda b, i: (blk0 + b * nq + i, 0)),
        out_shape=jax.ShapeDtypeStruct((N_TOK, MLA_HEADS * MLA_V), BF16),
        scratch_shapes=[pltpu.VMEM((lk, MLA_HEADS * LANES), BF16),
                        pltpu.VMEM((lk, MLA_HEADS * MLA_V), BF16)],
        input_output_aliases={4: 0},
        compiler_params=_params(("arbitrary", "arbitrary")),
        name="mla_attn_%d" % lk,
    )(q, ckv_all, kr_all, w_ukv_pad, o_prev)


def _post_kernel(mix_ref, x_ref, mod_ref, g_ref, wo_ref, rw_ref, rb_ref,
                 xmid_ref, h2_ref, route_ref, cnt_ref, base_ref):
    i = pl.program_id(0)

    @pl.when(i == 0)
    def _():
        base_ref[...] = jnp.zeros_like(base_ref)

    m = mod_ref[0]
    xm = x_ref[...] + m[2:3] * jnp.dot(mix_ref[...], wo_ref[...], preferred_element_type=F32)
    xmid_ref[...] = xm
    h2 = _rms(xm, g_ref[...]) * (1.0 + m[4:5]) + m[3:4]
    _store_slabs(h2_ref, 0, h2)
    logits = _dot3(h2, rw_ref[...]) + rb_ref[...]
    lane_f = lax.broadcasted_iota(jnp.int32, logits.shape, 1).astype(F32)
    l = logits
    vals, sels = [], []
    for _k in range(TOP_K):
        mx = jnp.max(l, axis=-1, keepdims=True)
        idx = jnp.min(jnp.where(l == mx, lane_f, float(N_EXPERTS)), axis=-1, keepdims=True)
        sel = lane_f == idx
        vals.append(mx)
        sels.append(sel)
        l = jnp.where(sel, -jnp.inf, l)
    ex = [jnp.exp(v - vals[0]) for v in vals]
    den = ex[0] + ex[1] + ex[2] + ex[3]
    onehot = (sels[0] | sels[1] | sels[2] | sels[3]).astype(F32)
    r_i = lax.broadcasted_iota(jnp.int32, (TM, TM), 0)
    c_i = lax.broadcasted_iota(jnp.int32, (TM, TM), 1)
    tri = (c_i < r_i).astype(BF16)
    before = base_ref[...] + jnp.dot(tri, onehot.astype(BF16), preferred_element_type=F32)
    out_lane = lax.broadcasted_iota(jnp.int32, (TM, LANES), 1)
    route = jnp.zeros((TM, LANES), F32)
    for kk in range(TOP_K):
        sel_f = sels[kk].astype(F32)
        eid = jnp.sum(sel_f * lane_f, axis=-1, keepdims=True)
        pos = jnp.sum(sel_f * before, axis=-1, keepdims=True)
        route = jnp.where(out_lane == kk, eid, route)
        route = jnp.where(out_lane == TOP_K + kk, ex[kk] / den, route)
        route = jnp.where(out_lane == 2 * TOP_K + kk, pos, route)
    route_ref[...] = route
    base_ref[...] = base_ref[...] + jnp.sum(onehot, axis=0, keepdims=True)
    cnt_ref[...] = base_ref[...]


def _post(mix, x, mod, g2, w_out, router_w, router_b):
    return pl.pallas_call(
        _post_kernel,
        grid=(N_TOK // TM,),
        in_specs=[pl.BlockSpec((TM, D_MODEL), lambda i: (i, 0)),
                  pl.BlockSpec((TM, D_MODEL), lambda i: (i, 0)),
                  pl.BlockSpec((1, 6, D_MODEL), lambda i: (_seg(i, TM), 0, 0)),
                  pl.BlockSpec((1, D_MODEL), lambda i: (0, 0)),
                  pl.BlockSpec((D_MODEL, D_MODEL), lambda i: (0, 0)),
                  pl.BlockSpec((D_MODEL, N_EXPERTS), lambda i: (0, 0)),
                  pl.BlockSpec((1, N_EXPERTS), lambda i: (0, 0))],
        out_specs=[pl.BlockSpec((TM, D_MODEL), lambda i: (i, 0)),
                   pl.BlockSpec((TM * SLAB, LANES), lambda i: (i, 0)),
                   pl.BlockSpec((TM, LANES), lambda i: (i, 0)),
                   pl.BlockSpec((1, N_EXPERTS), lambda i: (0, 0))],
        out_shape=[jax.ShapeDtypeStruct((N_TOK, D_MODEL), F32),
                   jax.ShapeDtypeStruct((N_TOK * SLAB, LANES), F32),
                   jax.ShapeDtypeStruct((N_TOK, LANES), F32),
                   jax.ShapeDtypeStruct((1, N_EXPERTS), F32)],
        scratch_shapes=[pltpu.VMEM((1, N_EXPERTS), F32)],
        compiler_params=_params(("arbitrary",)),
        name="post_router",
    )(mix, x, mod, g2.reshape(1, D_MODEL), w_out, router_w, router_b.reshape(1, N_EXPERTS))


def _route_plan(route, counts):
    eid = route[:, 0:TOP_K].astype(jnp.int32)
    gates = route[:, TOP_K:2 * TOP_K]
    pos = route[:, 2 * TOP_K:3 * TOP_K].astype(jnp.int32)
    cnt = counts[0].astype(jnp.int32)
    blocks = (cnt + TME - 1) // TME
    blk_end = jnp.cumsum(blocks)
    pad_start = (blk_end - blocks) * TME
    dest = pad_start[eid] + pos
    tok = jnp.broadcast_to(jnp.arange(N_TOK, dtype=jnp.int32)[:, None], dest.shape)
    buf_tok = jnp.zeros((P_ROWS,), jnp.int32).at[dest.reshape(-1)].set(tok.reshape(-1))
    n_used = blk_end[-1]
    blk_e = jnp.searchsorted(blk_end, jnp.arange(N_EBLK, dtype=jnp.int32), side='right')
    blk_e = jnp.minimum(blk_e, N_EXPERTS - 1).astype(jnp.int32)
    last_e = blk_e[jnp.maximum(n_used - 1, 0)]
    blk_e = jnp.where(jnp.arange(N_EBLK) < n_used, blk_e, last_e).astype(jnp.int32)
    return dest, gates, buf_tok, blk_e, n_used.reshape(1).astype(jnp.int32)


def _moe_kernel(blk_e_ref, nused_ref, tok_ref, tokn_ref, h2_hbm, win_ref, bin_ref, wout_ref, bout_ref,
                y_ref, xbuf, sem, win_bf, wout_bf):
    j = pl.program_id(0)
    n_used = nused_ref[0]
    slot = j % 2

    def gather(idx_ref, s, wait):
        def body(r, carry):
            cp = _slab_copy(h2_hbm, xbuf, sem.at[s], idx_ref[0, 0, r], s * TME + r)
            if wait:
                cp.wait()
            else:
                cp.start()
            return carry
        lax.fori_loop(0, TME, body, 0, unroll=8)

    @pl.when(j == 0)
    def _():
        gather(tok_ref, 0, False)

    @pl.when(j + 1 < n_used)
    def _():
        gather(tokn_ref, 1 - slot, False)

    @pl.when(j < n_used)
    def _():
        gather(tok_ref, slot, True)
        changed = jnp.logical_or(j == 0, blk_e_ref[j] != blk_e_ref[jnp.maximum(j - 1, 0)])

        @pl.when(changed)
        def _():
            win_bf[...] = win_ref[0].astype(BF16)
            wout_bf[...] = wout_ref[0].astype(BF16)

        xb = _load_slabs(xbuf, slot * TME, TME).astype(BF16)
        hb = jnp.dot(xb, win_bf[...], preferred_element_type=F32) + bin_ref[0]
        glu = jnp.minimum(hb[:, :D_EXPERT], SWIGLU_LIMIT)
        lin = jnp.clip(hb[:, D_EXPERT:], -SWIGLU_LIMIT, SWIGLU_LIMIT)
        act = glu * jax.nn.sigmoid(SWIGLU_ALPHA * glu) * (lin + 1.0)
        y = jnp.dot(act.astype(BF16), wout_bf[...], preferred_element_type=F32) + bout_ref[0]
        _store_slabs(y_ref, 0, y)

    @pl.when(j >= n_used)
    def _():
        y_ref[...] = jnp.zeros_like(y_ref)


def _moe(h2, buf_tok, blk_e, n_used, w_in, b_in, w_out, b_out):
    tok3 = buf_tok.reshape(N_EBLK, 1, TME)
    grid_spec = pltpu.PrefetchScalarGridSpec(
        num_scalar_prefetch=2,
        grid=(N_EBLK,),
        in_specs=[pl.BlockSpec((1, 1, TME), lambda j, be, nu: (j, 0, 0), memory_space=pltpu.SMEM),
                  pl.BlockSpec((1, 1, TME), lambda j, be, nu: (jnp.minimum(j + 1, N_EBLK - 1), 0, 0),
                               memory_space=pltpu.SMEM),
                  pl.BlockSpec(memory_space=pl.ANY),
                  pl.BlockSpec((1, D_MODEL, 2 * D_EXPERT), lambda j, be, nu: (be[j], 0, 0)),
                  pl.BlockSpec((1, 1, 2 * D_EXPERT), lambda j, be, nu: (be[j], 0, 0)),
                  pl.BlockSpec((1, D_EXPERT, D_MODEL), lambda j, be, nu: (be[j], 0, 0)),
                  pl.BlockSpec((1, 1, D_MODEL), lambda j, be, nu: (be[j], 0, 0))],
        out_specs=pl.BlockSpec((TME * SLAB, LANES), lambda j, be, nu: (j, 0)),
        scratch_shapes=[pltpu.VMEM((2 * TME * SLAB, LANES), F32),
                        pltpu.SemaphoreType.DMA((2,)),
                        pltpu.VMEM((D_MODEL, 2 * D_EXPERT), BF16),
                        pltpu.VMEM((D_EXPERT, D_MODEL), BF16)])
    return pl.pallas_call(
        _moe_kernel,
        grid_spec=grid_spec,
        out_shape=jax.ShapeDtypeStruct((P_ROWS * SLAB, LANES), F32),
        compiler_params=_params(("arbitrary",)),
        name="moe_ffn",
    )(blk_e, n_used, tok3, tok3, h2, w_in, b_in.reshape(N_EXPERTS, 1, -1), w_out,
      b_out.reshape(N_EXPERTS, 1, -1))


def _combine_kernel(final, dest_ref, destn_ref, y_hbm, xmid_ref, mod_ref, route_ref, fg_ref,
                    o_ref, buf, sem):
    i = pl.program_id(0)
    n = pl.num_programs(0)
    slot = i % 2

    def gather(idx_ref, s, wait):
        def body(r, carry):
            for kk in range(TOP_K):
                cp = _slab_copy(y_hbm, buf, sem.at[s], idx_ref[0, 0, r * TOP_K + kk],
                                (s * TOP_K + kk) * TMC + r)
                if wait:
                    cp.wait()
                else:
                    cp.start()
            return carry
        lax.fori_loop(0, TMC, body, 0, unroll=4)

    @pl.when(i == 0)
    def _():
        gather(dest_ref, 0, False)

    @pl.when(i + 1 < n)
    def _():
        gather(destn_ref, 1 - slot, False)

    gather(dest_ref, slot, True)
    gates = route_ref[...]
    acc = gates[:, TOP_K:TOP_K + 1] * _load_slabs(buf, slot * TOP_K * TMC, TMC)
    for kk in range(1, TOP_K):
        acc = acc + gates[:, TOP_K + kk:TOP_K + kk + 1] * _load_slabs(buf, (slot * TOP_K + kk) * TMC, TMC)
    out = xmid_ref[...] + mod_ref[0][5:6] * acc
    if final:
        out = _rms(out, fg_ref[...])
    o_ref[...] = out


def _combine(y, dest, xmid, mod, route, final_g, final):
    nt = N_TOK // TMC
    dest3 = dest.reshape(nt, 1, TMC * TOP_K)
    return pl.pallas_call(
        functools.partial(_combine_kernel, final),
        grid=(nt,),
        in_specs=[pl.BlockSpec((1, 1, TMC * TOP_K), lambda i: (i, 0, 0), memory_space=pltpu.SMEM),
                  pl.BlockSpec((1, 1, TMC * TOP_K), lambda i: (jnp.minimum(i + 1, nt - 1), 0, 0),
                               memory_space=pltpu.SMEM),
                  pl.BlockSpec(memory_space=pl.ANY),
                  pl.BlockSpec((TMC, D_MODEL), lambda i: (i, 0)),
                  pl.BlockSpec((1, 6, D_MODEL), lambda i: (_seg(i, TMC), 0, 0)),
                  pl.BlockSpec((TMC, LANES), lambda i: (i, 0)),
                  pl.BlockSpec((1, D_MODEL), lambda i: (0, 0))],
        out_specs=pl.BlockSpec((TMC, D_MODEL), lambda i: (i, 0)),
        out_shape=jax.ShapeDtypeStruct((N_TOK, D_MODEL), F32),
        scratch_shapes=[pltpu.VMEM((2 * TOP_K * TMC * SLAB, LANES), F32),
                        pltpu.SemaphoreType.DMA((2,))],
        compiler_params=_params(("arbitrary",)),
        name="moe_combine_final" if final else "moe_combine",
    )(dest3, dest3, y, xmid, mod, route, final_g.reshape(1, D_MODEL))


def _axial_tables(seq, dim, lanes, offset):
    half = dim // 2
    nf = half // 2
    t = jnp.arange(seq)
    row = (t // GRID_W).astype(F32)
    col = (t % GRID_W).astype(F32)
    inv = ROPE_BASE ** (-jnp.arange(nf, dtype=F32) / nf)
    d = jnp.arange(lanes) - offset
    inside = (d >= 0) & (d < dim)
    dd = jnp.clip(d, 0, dim - 1)
    w = dd % half
    pos = jnp.where((dd // half)[None, :] == 0, row[:, None], col[:, None])
    ang = pos * inv[w % nf][None, :]
    sign = jnp.where(w < nf, -1.0, 1.0)[None, :]
    cos = jnp.where(inside[None, :], jnp.cos(ang), 1.0)
    sins = jnp.where(inside[None, :], sign * jnp.sin(ang), 0.0)
    return cos.astype(F32), sins.astype(F32)


def kernel(x_prompt, x_sample, c, state_ret, state_ssm_re, state_ssm_im, cache_mla_ckv, cache_mla_krope, c_ctx, w_ada, b_ada, norm1_g, norm2_g, final_norm_g, w_in_ab, w_out_ab, ret_decay_logit, ssm_a_re, ssm_a_im, ssm_log_dt, ssm_b_re, ssm_b_im, ssm_c_re, ssm_c_im, ssm_d, ssm_w_glu, w_in_c, mla_q_norm_g, mla_kv_norm_g, mla_w_uq, mla_w_ukv, w_out_c, router_w, router_b, moe_w_in, moe_b_in, moe_w_out, moe_b_out):
    x = jnp.concatenate([x_prompt.reshape(NP, D_MODEL), x_sample.reshape(NS, D_MODEL)], axis=0)
    mods = _modulation(c_ctx, c, w_ada, b_ada)

    rc, rs = _axial_tables(DEC_SEQ, RET_QK_DIM, RET_QK_DIM, 0)
    ret_rope = (jnp.tile(rc, (1, RET_HEADS)), jnp.tile(rs, (1, RET_HEADS)))
    mc, ms = _axial_tables(DEC_SEQ, MLA_ROPE, LANES, MLA_NOPE)
    mla_cos = jnp.concatenate([jnp.ones((TM, LANES), F32), mc], axis=0)
    mla_sin = jnp.concatenate([jnp.zeros((TM, LANES), F32), ms], axis=0)

    ret_l, ssm_re_l, ssm_im_l, ckv_l, kr_l = [], [], [], [], []
    for l in range(DEPTH):
        i = l // 2
        mod = mods[l]
        if l % 2 == 0:
            qkvg, u = _pre_ab(x, norm1_g[l], mod, w_in_ab[i].astype(BF16))
            w1, w2, a16 = _s5_tables(ssm_a_re[i], ssm_a_im[i], ssm_log_dt[i], ssm_b_re[i], ssm_b_im[i],
                                     ssm_c_re[i], ssm_c_im[i])
            x0_p = jnp.zeros((SSM_GROUPS, 2, BATCH, LANES), F32)
            x0_s = jnp.stack(
                [state_ssm_re[:, i].transpose(2, 0, 1, 3).reshape(SSM_GROUPS, DEC_BATCH, LANES),
                 state_ssm_im[:, i].transpose(2, 0, 1, 3).reshape(SSM_GROUPS, DEC_BATCH, LANES)], axis=1)
            yp, xf_p = _s5(_to_chunks(u[:NP], BATCH, SEQ), w1, w2, a16, x0_p, BATCH, SEQ // SSM_CHUNK)
            ys, _ = _s5(_to_chunks(u[NP:], DEC_BATCH, DEC_SEQ), w1, w2, a16, x0_s, DEC_BATCH,
                        DEC_SEQ // SSM_CHUNK)
            dsum, dec, cd = _ret_tables(ret_decay_logit[i])
            mix0 = jnp.zeros((N_TOK, D_MODEL), BF16)
            mix, s_ret = _ab_core(qkvg, u, _from_chunks(yp, BATCH, SEQ), dsum, dec, cd, ssm_d[i],
                                  ssm_w_glu[i].astype(BF16), mix0, nb=BATCH, seq=SEQ, row0=0)
            (mix,) = _ab_core(qkvg, u, _from_chunks(ys, DEC_BATCH, DEC_SEQ), dsum, dec, cd, ssm_d[i],
                              ssm_w_glu[i].astype(BF16), mix, nb=DEC_BATCH, seq=DEC_SEQ, row0=NP,
                              rope=ret_rope, s0=state_ret[:, i])
            ret_l.append(s_ret)
            xf = xf_p.reshape(SSM_GROUPS, 2, BATCH, 2, SSM_STATE).transpose(1, 2, 3, 0, 4)
            ssm_re_l.append(xf[0])
            ssm_im_l.append(xf[1])
            w_out = w_out_ab[i]
        else:
            w_c = w_in_c[i]
            w_in_pad = jnp.concatenate(
                [w_c[:, :MLA_Q_RANK + MLA_KV_RANK], jnp.zeros((D_MODEL, MLA_NOPE), F32),
                 w_c[:, MLA_Q_RANK + MLA_KV_RANK:], jnp.zeros((D_MODEL, LANES - MLA_NOPE - MLA_ROPE), F32)],
                axis=1).astype(BF16)
            wq = mla_w_uq[i].reshape(MLA_Q_RANK, MLA_HEADS, MLA_NOPE + MLA_ROPE)
            w_uq_pad = jnp.pad(wq, ((0, 0), (0, 0), (0, LANES - MLA_NOPE - MLA_ROPE))).reshape(
                MLA_Q_RANK, MLA_HEADS * LANES).astype(BF16)
            wkv = mla_w_ukv[i].reshape(MLA_KV_RANK, MLA_HEADS, MLA_NOPE + MLA_V)
            wk = jnp.pad(wkv[:, :, :MLA_NOPE], ((0, 0), (0, 0), (0, LANES - MLA_NOPE))).reshape(MLA_KV_RANK, -1)
            wv = wkv[:, :, MLA_NOPE:].reshape(MLA_KV_RANK, -1)
            w_ukv_pad = jnp.concatenate([wk, wv], axis=1).astype(BF16)
            q, ckv, kr, krp = _pre_c(x, norm1_g[l], mod, w_in_pad, mla_q_norm_g[i], mla_kv_norm_g[i],
                                     w_uq_pad, mla_cos, mla_sin)
            ckv_p = ckv[:NP].reshape(BATCH, SEQ, MLA_KV_RANK)
            kr_p = kr[:NP].reshape(BATCH, SEQ, LANES)
            ckv_s = jnp.concatenate([cache_mla_ckv[:, i], ckv[NP:].reshape(DEC_BATCH, DEC_SEQ, MLA_KV_RANK)],
                                    axis=1)
            cache_kr = jnp.pad(cache_mla_krope[:, i], ((0, 0), (0, 0), (MLA_NOPE, LANES - MLA_NOPE - MLA_ROPE)))
            kr_s = jnp.concatenate([cache_kr, krp[NP:].reshape(DEC_BATCH, DEC_SEQ, LANES)], axis=1)
            mix0 = jnp.zeros((N_TOK, D_MODEL), BF16)
            mix = _attention(q, ckv_p, kr_p, w_ukv_pad, mix0, nb=BATCH, seq=SEQ, lk=SEQ, row0=0)
            mix = _attention(q, ckv_s, kr_s, w_ukv_pad, mix, nb=DEC_BATCH, seq=DEC_SEQ,
                             lk=PAST_LEN + DEC_SEQ, row0=NP)
            ckv_l.append(ckv_p)
            kr_l.append(kr_p[:, :, MLA_NOPE:MLA_NOPE + MLA_ROPE])
            w_out = w_out_c[i]
        xmid, h2, route, counts = _post(mix, x, mod, norm2_g[l], w_out.astype(BF16), router_w[l], router_b[l])
        dest, _, buf_tok, blk_e, n_used = _route_plan(route, counts)
        y = _moe(h2, buf_tok, blk_e, n_used, moe_w_in[l], moe_b_in[l], moe_w_out[l], moe_b_out[l])
        x = _combine(y, dest, xmid, mod, route, final_norm_g, final=(l == DEPTH - 1))

    y_prompt = x[:NP].reshape(BATCH, SEQ, D_MODEL)
    y_sample = x[NP:].reshape(DEC_BATCH, DEC_SEQ, D_MODEL)
    return (y_prompt, y_sample, jnp.stack(ret_l, axis=1), jnp.stack(ssm_re_l, axis=1),
            jnp.stack(ssm_im_l, axis=1), jnp.stack(ckv_l, axis=1), jnp.stack(kr_l, axis=1))
```

```python
import functools
import math

import jax
import jax.numpy as jnp
from jax import lax
from jax.experimental import pallas as pl
from jax.experimental.pallas import tpu as pltpu

F32 = jnp.float32
BF16 = jnp.bfloat16
HIGHEST = lax.Precision.HIGHEST

D_MODEL = 1024
BATCH = 32
SEQ = 256
DEPTH = 4
DEC_BATCH = 8
DEC_SEQ = 1024
PAST_LEN = 512
GRID_W = 64
RET_HEADS = 4
RET_V_DIM = 128
RET_QK_DIM = 64
RET_CHUNK = 128
SSM_WIDTH = 512
SSM_GROUP = 16
SSM_GROUPS = 32
SSM_STATE = 64
SSM_CHUNK = 16
MLA_HEADS = 8
MLA_NOPE = 64
MLA_ROPE = 32
MLA_V = 128
MLA_Q_RANK = 256
MLA_KV_RANK = 128
N_EXPERTS = 32
TOP_K = 4
D_EXPERT = 1024
SWIGLU_LIMIT = 7.0
SWIGLU_ALPHA = 1.702
ROPE_BASE = 10000.0
NORM_EPS = 1e-6

NP = BATCH * SEQ
NS = DEC_BATCH * DEC_SEQ
N_TOK = NP + NS
N_SEG = 1 + DEC_BATCH
LANES = 128

TM = 512
TMC = 256
TME = 512
N_ASSIGN = N_TOK * TOP_K
N_EBLK = N_ASSIGN // TME + N_EXPERTS
P_ROWS = N_EBLK * TME
VMEM_LIMIT = 56 * 1024 * 1024


def _params(sem, vmem=VMEM_LIMIT):
    return pltpu.CompilerParams(dimension_semantics=sem, vmem_limit_bytes=vmem)


def _bdot(a, b):
    return jnp.dot(a.astype(BF16), b.astype(BF16), preferred_element_type=F32)


def _split(x):
    hi = x.astype(BF16)
    lo = (x - hi.astype(F32)).astype(BF16)
    return hi, lo


def _dot3(a, b):
    ah, al = _split(a)
    bh, bl = _split(b)
    d = functools.partial(jnp.dot, preferred_element_type=F32)
    return d(ah, bh) + d(al, bh) + d(ah, bl)


def _rms(x, g):
    return x * lax.rsqrt(jnp.mean(x * x, axis=-1, keepdims=True) + NORM_EPS) * g


SLAB = D_MODEL // LANES


def _store_slabs(ref, row0, val):
    m = val.shape[0]
    for s in range(SLAB):
        ref[pl.ds(row0 * SLAB + s, m, stride=SLAB), :] = val[:, s * LANES:(s + 1) * LANES]


def _load_slabs(ref, row0, m):
    return jnp.concatenate([ref[pl.ds(row0 * SLAB + s, m, stride=SLAB), :] for s in range(SLAB)], axis=-1)


def _slab_copy(src, dst, sem, src_row, dst_row):
    return pltpu.make_async_copy(src.at[pl.ds(pl.multiple_of(src_row * SLAB, SLAB), SLAB)],
                                 dst.at[pl.ds(pl.multiple_of(dst_row * SLAB, SLAB), SLAB)], sem)


def _seg(i, tm):
    return jnp.where(i < NP // tm, 0, 1 + (i * tm - NP) // DEC_SEQ)


def _mod_kernel(c_ref, w_ref, b_ref, o_ref):
    c = c_ref[...]
    s = c * jax.nn.sigmoid(c)
    o_ref[0] = _dot3(s, w_ref[0]) + b_ref[0]


def _modulation(c_ctx, c, w_ada, b_ada):
    rows = 16
    cvec = jnp.zeros((rows, D_MODEL), F32).at[0].set(c_ctx).at[1:N_SEG].set(c)
    nb = 1536
    out = pl.pallas_call(
        _mod_kernel,
        grid=(DEPTH, 6 * D_MODEL // nb),
        in_specs=[pl.BlockSpec((rows, D_MODEL), lambda l, j: (0, 0)),
                  pl.BlockSpec((1, D_MODEL, nb), lambda l, j: (l, 0, j)),
                  pl.BlockSpec((1, 1, nb), lambda l, j: (l, 0, j))],
        out_specs=pl.BlockSpec((1, rows, nb), lambda l, j: (l, 0, j)),
        out_shape=jax.ShapeDtypeStruct((DEPTH, rows, 6 * D_MODEL), F32),
        compiler_params=_params(("arbitrary", "arbitrary")),
        name="adaln_mod",
    )(cvec, w_ada, b_ada.reshape(DEPTH, 1, 6 * D_MODEL))
    return out[:, :N_SEG].reshape(DEPTH, N_SEG, 6, D_MODEL)


def _pre_ab_kernel(x_ref, g_ref, mod_ref, w_ref, qkvg_ref, u_ref):
    m = mod_ref[0]
    h = _rms(x_ref[...], g_ref[...]) * (1.0 + m[1:2]) + m[0:1]
    p = _bdot(h, w_ref[...])
    qkvg_ref[...] = p[:, :1536]
    for k in range(SSM_WIDTH // LANES):
        u_ref[k] = p[:, 1536 + k * LANES:1536 + (k + 1) * LANES]


def _pre_ab(x, g1, mod, w_in):
    return pl.pallas_call(
        _pre_ab_kernel,
        grid=(N_TOK // TM,),
        in_specs=[pl.BlockSpec((TM, D_MODEL), lambda i: (i, 0)),
                  pl.BlockSpec((1, D_MODEL), lambda i: (0, 0)),
                  pl.BlockSpec((1, 6, D_MODEL), lambda i: (_seg(i, TM), 0, 0)),
                  pl.BlockSpec((D_MODEL, 2048), lambda i: (0, 0))],
        out_specs=[pl.BlockSpec((TM, 1536), lambda i: (i, 0)),
                   pl.BlockSpec((SSM_WIDTH // LANES, TM, LANES), lambda i: (0, i, 0))],
        out_shape=[jax.ShapeDtypeStruct((N_TOK, 1536), F32),
                   jax.ShapeDtypeStruct((SSM_WIDTH // LANES, N_TOK, LANES), F32)],
        compiler_params=_params(("arbitrary",)),
        name="pre_ab",
    )(x, g1.reshape(1, D_MODEL), mod, w_in)


def _rope_lanes(x, cos, sins, first, shift):
    w = x.shape[-1]
    partner = jnp.where(first, pltpu.roll(x, w - shift, 1), pltpu.roll(x, shift, 1))
    return x * cos + partner * sins


def _pre_c_kernel(x_ref, g_ref, mod_ref, w_ref, qg_ref, kvg_ref, wuq_ref, cos_ref, sin_ref,
                  q_ref, ckv_ref, kr_ref, krp_ref):
    m = mod_ref[0]
    h = _rms(x_ref[...], g_ref[...]) * (1.0 + m[1:2]) + m[0:1]
    p = _bdot(h, w_ref[...])
    cq = _rms(p[:, :MLA_Q_RANK], qg_ref[...])
    ckv_ref[...] = _rms(p[:, MLA_Q_RANK:MLA_Q_RANK + MLA_KV_RANK], kvg_ref[...])
    kr = p[:, MLA_Q_RANK + MLA_KV_RANK:]
    kr_ref[...] = kr
    cos = cos_ref[...]
    sins = sin_ref[...]
    lane = lax.broadcasted_iota(jnp.int32, cos.shape, 1)
    first = (lane % 16) < 8
    krp_ref[...] = _rope_lanes(kr, cos, sins, first, 8)
    q = _bdot(cq, wuq_ref[...])
    for hh in range(MLA_HEADS):
        sl = slice(hh * LANES, (hh + 1) * LANES)
        q_ref[:, sl] = _rope_lanes(q[:, sl], cos, sins, first, 8).astype(BF16)


def _pre_c(x, g1, mod, w_in_pad, q_g, kv_g, w_uq_pad, cos_t, sin_t):
    def tab(i):
        return (jnp.where(i < NP // TM, 0, 1 + ((i * TM - NP) % DEC_SEQ) // TM), 0)
    return pl.pallas_call(
        _pre_c_kernel,
        grid=(N_TOK // TM,),
        in_specs=[pl.BlockSpec((TM, D_MODEL), lambda i: (i, 0)),
                  pl.BlockSpec((1, D_MODEL), lambda i: (0, 0)),
                  pl.BlockSpec((1, 6, D_MODEL), lambda i: (_seg(i, TM), 0, 0)),
                  pl.BlockSpec((D_MODEL, 512), lambda i: (0, 0)),
                  pl.BlockSpec((1, MLA_Q_RANK), lambda i: (0, 0)),
                  pl.BlockSpec((1, MLA_KV_RANK), lambda i: (0, 0)),
                  pl.BlockSpec((MLA_Q_RANK, MLA_HEADS * LANES), lambda i: (0, 0)),
                  pl.BlockSpec((TM, LANES), tab),
                  pl.BlockSpec((TM, LANES), tab)],
        out_specs=[pl.BlockSpec((TM, MLA_HEADS * LANES), lambda i: (i, 0)),
                   pl.BlockSpec((TM, MLA_KV_RANK), lambda i: (i, 0)),
                   pl.BlockSpec((TM, LANES), lambda i: (i, 0)),
                   pl.BlockSpec((TM, LANES), lambda i: (i, 0))],
        out_shape=[jax.ShapeDtypeStruct((N_TOK, MLA_HEADS * LANES), BF16),
                   jax.ShapeDtypeStruct((N_TOK, MLA_KV_RANK), F32),
                   jax.ShapeDtypeStruct((N_TOK, LANES), F32),
                   jax.ShapeDtypeStruct((N_TOK, LANES), F32)],
        compiler_params=_params(("arbitrary",)),
        name="pre_c",
    )(x, g1.reshape(1, D_MODEL), mod, w_in_pad, q_g.reshape(1, -1), kv_g.reshape(1, -1), w_uq_pad,
      cos_t, sin_t)


def _s5_tables(a_re, a_im, log_dt, b_re, b_im, c_re, c_im):
    C = SSM_CHUNK
    a_re = jnp.minimum(a_re.astype(F32), -1e-4)
    a_im = a_im.astype(F32)
    dt = jnp.exp(log_dt.astype(F32))[..., None]
    mag = jnp.exp(dt * a_re)
    ab_re = mag * jnp.cos(dt * a_im)
    ab_im = mag * jnp.sin(dt * a_im)
    den = a_re * a_re + a_im * a_im
    f_re = ((ab_re - 1.0) * a_re + ab_im * a_im) / den
    f_im = (ab_im * a_re - (ab_re - 1.0) * a_im) / den
    b_re = b_re.astype(F32)
    b_im = b_im.astype(F32)
    bb_re = f_re[..., None] * b_re - f_im[..., None] * b_im
    bb_im = f_re[..., None] * b_im + f_im[..., None] * b_re
    ks = jnp.arange(C + 1, dtype=F32)[:, None, None, None]
    pmag = jnp.exp(ks * (dt * a_re)[None])
    p_re = pmag * jnp.cos(ks * (dt * a_im)[None])
    p_im = pmag * jnp.sin(ks * (dt * a_im)[None])
    c_re = c_re.astype(F32)[None]
    c_im = c_im.astype(F32)[None]
    cl_re = c_re * p_re[:, :, :, None, :] - c_im * p_im[:, :, :, None, :]
    cl_im = c_re * p_im[:, :, :, None, :] + c_im * p_re[:, :, :, None, :]
    kmat = (jnp.einsum('kdgpn,dgnq->kdgpq', cl_re[:C], bb_re, precision=HIGHEST)
            - jnp.einsum('kdgpn,dgnq->kdgpq', cl_im[:C], bb_im, precision=HIGHEST))
    s_idx = jnp.arange(C)[:, None]
    j_idx = jnp.arange(C)[None, :]

    def toeplitz(k, diff):
        t = k[jnp.clip(diff, 0, C - 1)]
        t = jnp.where((diff >= 0)[:, :, None, None, None], t, 0.0)
        return t.transpose(2, 0, 4, 1, 3).reshape(SSM_GROUPS, C * SSM_GROUP, C * SSM_GROUP)

    m_loc = toeplitz(kmat[:, 0], j_idx - s_idx) + toeplitz(kmat[:, 1], s_idx - j_idx)

    def state_in(d, expo):
        pr = p_re[expo, d][..., None]
        pi = p_im[expo, d][..., None]
        er = pr * bb_re[d][None] - pi * bb_im[d][None]
        ei = pr * bb_im[d][None] + pi * bb_re[d][None]
        to = lambda e: e.transpose(1, 0, 3, 2).reshape(SSM_GROUPS, C * SSM_GROUP, SSM_STATE)
        return to(er), to(ei)

    ef_re, ef_im = state_in(0, C - 1 - jnp.arange(C))
    eb_re, eb_im = state_in(1, jnp.arange(C))
    w1 = jnp.concatenate([m_loc, ef_re, eb_re, ef_im, eb_im], axis=-1)

    def state_out(d, expo):
        to = lambda e: e.transpose(1, 3, 0, 2).reshape(SSM_GROUPS, SSM_STATE, C * SSM_GROUP)
        return to(cl_re[expo, d]), to(-cl_im[expo, d])

    ff_re, ff_im = state_out(0, 1 + jnp.arange(C))
    fb_re, fb_im = state_out(1, C - jnp.arange(C))
    w2 = jnp.concatenate([ff_re, fb_re, ff_im, fb_im], axis=1)
    a16 = jnp.stack([jnp.concatenate([p_re[C, 0], p_re[C, 1]], axis=-1),
                     jnp.concatenate([p_im[C, 0], p_im[C, 1]], axis=-1)], axis=1)
    return w1.astype(BF16), w2.astype(BF16), a16


S5_GB = LANES // SSM_GROUP


def _s5_kernel(nb, nc, u_ref, w1_ref, w2_ref, a_ref, x0_ref, y_ref, xf_ref,
               ug_ref, yl_ref, lre_ref, lim_ref, s_ref, l2re_ref, l2im_ref):
    r = nb * nc
    for s in range(SSM_CHUNK):
        piece = u_ref[pl.ds(s, r, stride=SSM_CHUNK), :]
        for gl in range(S5_GB):
            ug_ref[gl, :, s * SSM_GROUP:(s + 1) * SSM_GROUP] = piece[:, gl * SSM_GROUP:(gl + 1) * SSM_GROUP]
    fwd = lax.broadcasted_iota(jnp.int32, (nb, LANES), 1) < SSM_STATE

    def group(gl, carry):
        r1 = _bdot(ug_ref[gl], w1_ref[gl])
        yl_ref[gl] = r1[:, 0:256]
        lre_ref[...] = r1[:, 256:384]
        lim_ref[...] = r1[:, 384:512]
        a = a_ref[gl]
        a_re = a[0:1]
        a_im = a[1:2]
        s_re = x0_ref[gl, 0]
        s_im = x0_ref[gl, 1]
        s_ref[0, 0] = s_re
        s_ref[1, 0] = s_im
        for k in range(nc):
            rf = pl.ds(k, nb, stride=nc)
            rb = pl.ds(nc - 1 - k, nb, stride=nc)
            l_re = jnp.where(fwd, lre_ref[rf, :], lre_ref[rb, :])
            l_im = jnp.where(fwd, lim_ref[rf, :], lim_ref[rb, :])
            n_re = a_re * s_re - a_im * s_im + l_re
            n_im = a_re * s_im + a_im * s_re + l_im
            s_re, s_im = n_re, n_im
            s_ref[0, k + 1] = s_re
            s_ref[1, k + 1] = s_im
        xf_ref[gl, 0] = s_re
        xf_ref[gl, 1] = s_im
        for c in range(nc):
            rows = pl.ds(c, nb, stride=nc)
            l2re_ref[rows, :] = jnp.where(fwd, s_ref[0, c], s_ref[0, nc - 1 - c])
            l2im_ref[rows, :] = jnp.where(fwd, s_ref[1, c], s_ref[1, nc - 1 - c])
        l2 = jnp.concatenate([l2re_ref[...], l2im_ref[...]], axis=-1)
        yl_ref[gl] = yl_ref[gl] + _bdot(l2, w2_ref[gl])
        return carry

    lax.fori_loop(0, S5_GB, group, 0)
    for j in range(SSM_CHUNK):
        tile = jnp.concatenate([yl_ref[gl, :, j * SSM_GROUP:(j + 1) * SSM_GROUP] for gl in range(S5_GB)],
                               axis=-1)
        y_ref[pl.ds(j, r, stride=SSM_CHUNK), :] = tile


def _s5(u4, w1, w2, a16, x0, nb, nc, grp):
    r = nb * nc
    t = r * SSM_CHUNK
    nblk = SSM_WIDTH // LANES
    return pl.pallas_call(
        functools.partial(_s5_kernel, nb, nc),
        grid=(nblk,),
        in_specs=[pl.BlockSpec((None, t, LANES), lambda cb: (cb, grp, 0)),
                  pl.BlockSpec((S5_GB, 256, 512), lambda cb: (cb, 0, 0)),
                  pl.BlockSpec((S5_GB, 256, 256), lambda cb: (cb, 0, 0)),
                  pl.BlockSpec((S5_GB, 2, LANES), lambda cb: (cb, 0, 0)),
                  pl.BlockSpec((S5_GB, 2, nb, LANES), lambda cb: (cb, 0, 0, 0))],
        out_specs=[pl.BlockSpec((None, t, LANES), lambda cb: (cb, 0, 0)),
                   pl.BlockSpec((S5_GB, 2, nb, LANES), lambda cb: (cb, 0, 0, 0))],
        out_shape=[jax.ShapeDtypeStruct((nblk, t, LANES), F32),
                   jax.ShapeDtypeStruct((SSM_GROUPS, 2, nb, LANES), F32)],
        scratch_shapes=[pltpu.VMEM((S5_GB, r, 256), F32),
                        pltpu.VMEM((S5_GB, r, 256), F32),
                        pltpu.VMEM((r, LANES), F32),
                        pltpu.VMEM((r, LANES), F32),
                        pltpu.VMEM((2, nc + 1, nb, LANES), F32),
                        pltpu.VMEM((r, LANES), F32),
                        pltpu.VMEM((r, LANES), F32)],
        compiler_params=_params(("arbitrary",)),
        name="s5_chunked",
    )(u4, w1, w2, a16, x0)


def _ret_tables(logit):
    C = RET_CHUNK
    lg = jax.nn.log_sigmoid(logit.astype(F32))
    pos = jnp.arange(C, dtype=F32)
    diff = pos[:, None] - pos[None, :]
    dsum = (jnp.where(diff >= 0, jnp.exp(jnp.maximum(diff, 0.0)[None] * lg[0][:, None, None]), 0.0)
            + jnp.where(diff <= 0, jnp.exp(jnp.maximum(-diff, 0.0)[None] * lg[1][:, None, None]), 0.0))
    kdf = jnp.exp((C - 1 - pos)[:, None] * lg[0][None, :])
    kdb = jnp.exp(pos[:, None] * lg[1][None, :])
    qdf = jnp.exp((pos + 1)[:, None] * lg[0][None, :])
    qdb = jnp.exp((C - pos)[:, None] * lg[1][None, :])
    dec = jnp.stack([jnp.repeat(t, RET_QK_DIM, axis=1) for t in (kdf, kdb, qdf, qdb)])
    cd = jnp.exp(C * lg)
    return dsum, dec, cd


def _ab_core_kernel(seq, use_pos, with_state, *refs):
    it = iter(refs)
    qkvg_ref, u_ref, ycv_ref, dsum_ref, dec_ref, cd_ref, d_ref, wglu_ref = [next(it) for _ in range(8)]
    if use_pos:
        cos_ref, sin_ref = next(it), next(it)
    s0_ref = next(it) if use_pos else None
    next(it)
    mix_ref = next(it)
    st_ref = next(it) if with_state else None
    C = RET_CHUNK
    n = seq // C
    q = qkvg_ref[:, 0:256] * (RET_QK_DIM ** -0.5)
    k = qkvg_ref[:, 256:512]
    if use_pos:
        cos = cos_ref[...]
        sins = sin_ref[...]
        first = (lax.broadcasted_iota(jnp.int32, cos.shape, 1) % 32) < 16
        q = _rope_lanes(q, cos, sins, first, 16)
        k = _rope_lanes(k, cos, sins, first, 16)
    kdf, kdb, qdf, qdb = dec_ref[0], dec_ref[1], dec_ref[2], dec_ref[3]
    for h in range(RET_HEADS):
        qs = slice(h * RET_QK_DIM, (h + 1) * RET_QK_DIM)
        vs = slice(512 + h * RET_V_DIM, 512 + (h + 1) * RET_V_DIM)
        gs = slice(1024 + h * RET_V_DIM, 1024 + (h + 1) * RET_V_DIM)
        qc, kc, vc, kvf, kvb = [], [], [], [], []
        for c in range(n):
            rows = slice(c * C, (c + 1) * C)
            qc.append(q[rows, qs])
            kc.append(k[rows, qs])
            vc.append(qkvg_ref[rows, vs])
            kvf.append(_bdot((kc[c] * kdf[:, qs]).T, vc[c]))
            kvb.append(_bdot((kc[c] * kdb[:, qs]).T, vc[c]))
        if use_pos:
            sf = s0_ref[0, 0, h]
            sb = s0_ref[0, 1, h]
        else:
            sf = jnp.zeros((RET_QK_DIM, RET_V_DIM), F32)
            sb = jnp.zeros((RET_QK_DIM, RET_V_DIM), F32)
        prev_f = []
        for c in range(n):
            prev_f.append(sf)
            sf = cd_ref[0, h] * sf + kvf[c]
        next_b = [None] * n
        for c in range(n - 1, -1, -1):
            next_b[c] = sb
            sb = cd_ref[1, h] * sb + kvb[c]
        if with_state:
            st_ref[0, 0, h] = sf
            st_ref[0, 1, h] = sb
        for c in range(n):
            rows = slice(c * C, (c + 1) * C)
            sc = lax.dot_general(qc[c].astype(BF16), kc[c].astype(BF16), (((1,), (1,)), ((), ())),
                                 preferred_element_type=F32) * dsum_ref[h]
            o = (_bdot(sc, vc[c]) + _bdot(qc[c] * qdf[:, qs], prev_f[c])
                 + _bdot(qc[c] * qdb[:, qs], next_b[c]))
            o = o * lax.rsqrt(jnp.mean(o * o, axis=-1, keepdims=True) + NORM_EPS)
            g = qkvg_ref[rows, gs]
            mix_ref[rows, h * RET_V_DIM:(h + 1) * RET_V_DIM] = (o * (g * jax.nn.sigmoid(g))).astype(BF16)
    nblk = SSM_WIDTH // LANES
    u = jnp.concatenate([u_ref[kb] for kb in range(nblk)], axis=-1)
    ycv = jnp.concatenate([ycv_ref[kb] for kb in range(nblk)], axis=-1)
    y = ycv + d_ref[...] * u
    z = 0.5 * y * (1.0 + jnp.tanh(math.sqrt(2.0 / math.pi) * (y + 0.044715 * (y * y * y))))
    z = z * jax.nn.sigmoid(_bdot(z, wglu_ref[...]))
    mix_ref[:, 512:1024] = z.astype(BF16)


def _ab_core(qkvg, u, ycv, dsum, dec, cd, d, w_glu, mix_prev, *, nb, seq, row0, rope=None, s0=None):
    use_pos = rope is not None
    with_state = not use_pos
    blk0 = row0 // seq
    in_specs = [pl.BlockSpec((seq, 1536), lambda b: (blk0 + b, 0)),
                pl.BlockSpec((SSM_WIDTH // LANES, seq, LANES), lambda b: (0, blk0 + b, 0)),
                pl.BlockSpec((SSM_WIDTH // LANES, seq, LANES), lambda b: (0, b, 0)),
                pl.BlockSpec((RET_HEADS, RET_CHUNK, RET_CHUNK), lambda b: (0, 0, 0)),
                pl.BlockSpec((4, RET_CHUNK, 256), lambda b: (0, 0, 0)),
                pl.BlockSpec(memory_space=pltpu.SMEM),
                pl.BlockSpec((1, SSM_WIDTH), lambda b: (0, 0)),
                pl.BlockSpec((SSM_WIDTH, SSM_WIDTH), lambda b: (0, 0))]
    args = [qkvg, u, ycv, dsum, dec, cd, d.reshape(1, SSM_WIDTH), w_glu]
    if use_pos:
        in_specs += [pl.BlockSpec((seq, 256), lambda b: (0, 0)),
                     pl.BlockSpec((seq, 256), lambda b: (0, 0)),
                     pl.BlockSpec((1, 2, RET_HEADS, RET_QK_DIM, RET_V_DIM), lambda b: (b, 0, 0, 0, 0))]
        args += [rope[0], rope[1], s0]
    in_specs.append(pl.BlockSpec(memory_space=pl.ANY))
    args.append(mix_prev)
    out_specs = [pl.BlockSpec((seq, D_MODEL), lambda b: (blk0 + b, 0))]
    out_shape = [jax.ShapeDtypeStruct((N_TOK, D_MODEL), BF16)]
    if with_state:
        out_specs.append(pl.BlockSpec((1, 2, RET_HEADS, RET_QK_DIM, RET_V_DIM), lambda b: (b, 0, 0, 0, 0)))
        out_shape.append(jax.ShapeDtypeStruct((nb, 2, RET_HEADS, RET_QK_DIM, RET_V_DIM), F32))
    return pl.pallas_call(
        functools.partial(_ab_core_kernel, seq, use_pos, with_state),
        grid=(nb,),
        in_specs=in_specs,
        out_specs=out_specs,
        out_shape=out_shape,
        input_output_aliases={len(args) - 1: 0},
        compiler_params=_params(("arbitrary",)),
        name="ab_core_pos" if use_pos else "ab_core",
    )(*args)


def _attn_kernel(lk, tq, q_ref, ckv_ref, kr_ref, wukv_ref, _, o_ref, k_sc, v_sc):
    @pl.when(pl.program_id(1) == 0)
    def _():
        kv = _bdot(ckv_ref[0], wukv_ref[...])
        kr = kr_ref[0]
        for h in range(MLA_HEADS):
            sl = slice(h * LANES, (h + 1) * LANES)
            k_sc[:, sl] = (kv[:, sl] + kr).astype(BF16)
        v_sc[...] = kv[:, MLA_HEADS * LANES:].astype(BF16)

    scale = (MLA_NOPE + MLA_ROPE) ** -0.5
    for h in range(MLA_HEADS):
        sl = slice(h * LANES, (h + 1) * LANES)
        s = lax.dot_general(q_ref[:, sl], k_sc[:, sl], (((1,), (1,)), ((), ())),
                            preferred_element_type=F32) * scale
        s = s - jnp.max(s, axis=-1, keepdims=True)
        e = jnp.exp(s)
        p = e / jnp.sum(e, axis=-1, keepdims=True)
        o_ref[:, sl] = _bdot(p, v_sc[:, sl]).astype(BF16)


def _attention(q, ckv_all, kr_all, w_ukv_pad, o_prev, *, nb, seq, lk, row0, tq=256):
    nq = seq // tq
    blk0 = row0 // tq
    return pl.pallas_call(
        functools.partial(_attn_kernel, lk, tq),
        grid=(nb, nq),
        in_specs=[pl.BlockSpec((tq, MLA_HEADS * LANES), lambda b, i: (blk0 + b * nq + i, 0)),
                  pl.BlockSpec((1, lk, MLA_KV_RANK), lambda b, i: (b, 0, 0)),
                  pl.BlockSpec((1, lk, LANES), lambda b, i: (b, 0, 0)),
                  pl.BlockSpec((MLA_KV_RANK, 2 * MLA_HEADS * LANES), lambda b, i: (0, 0)),
                  pl.BlockSpec(memory_space=pl.ANY)],
        out_specs=pl.BlockSpec((tq, MLA_HEADS * MLA_V), lambda b, i: (blk0 + b * nq + i, 0)),
        out_shape=jax.ShapeDtypeStruct((N_TOK, MLA_HEADS * MLA_V), BF16),
        scratch_shapes=[pltpu.VMEM((lk, MLA_HEADS * LANES), BF16),
                        pltpu.VMEM((lk, MLA_HEADS * MLA_V), BF16)],
        input_output_aliases={4: 0},
        compiler_params=_params(("arbitrary", "arbitrary")),
        name="mla_attn_%d" % lk,
    )(q, ckv_all, kr_all, w_ukv_pad, o_prev)


def _post_kernel(mix_ref, x_ref, mod_ref, g_ref, wo_ref, rw_ref, rb_ref,
                 xmid_ref, h2_ref, route_ref, cnt_ref, base_ref):
    i = pl.program_id(0)

    @pl.when(i == 0)
    def _():
        base_ref[...] = jnp.zeros_like(base_ref)

    m = mod_ref[0]
    xm = x_ref[...] + m[2:3] * jnp.dot(mix_ref[...], wo_ref[...], preferred_element_type=F32)
    xmid_ref[...] = xm
    h2 = _rms(xm, g_ref[...]) * (1.0 + m[4:5]) + m[3:4]
    _store_slabs(h2_ref, 0, h2)
    logits = _dot3(h2, rw_ref[...]) + rb_ref[...]
    lane_f = lax.broadcasted_iota(jnp.int32, logits.shape, 1).astype(F32)
    l = logits
    vals, sels = [], []
    for _k in range(TOP_K):
        mx = jnp.max(l, axis=-1, keepdims=True)
        idx = jnp.min(jnp.where(l == mx, lane_f, float(N_EXPERTS)), axis=-1, keepdims=True)
        sel = lane_f == idx
        vals.append(mx)
        sels.append(sel)
        l = jnp.where(sel, -jnp.inf, l)
    ex = [jnp.exp(v - vals[0]) for v in vals]
    den = ex[0] + ex[1] + ex[2] + ex[3]
    onehot = (sels[0] | sels[1] | sels[2] | sels[3]).astype(F32)
    r_i = lax.broadcasted_iota(jnp.int32, (TM, TM), 0)
    c_i = lax.broadcasted_iota(jnp.int32, (TM, TM), 1)
    tri = (c_i < r_i).astype(BF16)
    before = base_ref[...] + jnp.dot(tri, onehot.astype(BF16), preferred_element_type=F32)
    out_lane = lax.broadcasted_iota(jnp.int32, (TM, LANES), 1)
    route = jnp.zeros((TM, LANES), F32)
    for kk in range(TOP_K):
        sel_f = sels[kk].astype(F32)
        eid = jnp.sum(sel_f * lane_f, axis=-1, keepdims=True)
        pos = jnp.sum(sel_f * before, axis=-1, keepdims=True)
        route = jnp.where(out_lane == kk, eid, route)
        route = jnp.where(out_lane == TOP_K + kk, ex[kk] / den, route)
        route = jnp.where(out_lane == 2 * TOP_K + kk, pos, route)
    route_ref[...] = route
    base_ref[...] = base_ref[...] + jnp.sum(onehot, axis=0, keepdims=True)
    cnt_ref[...] = base_ref[...]


def _post(mix, x, mod, g2, w_out, router_w, router_b):
    return pl.pallas_call(
        _post_kernel,
        grid=(N_TOK // TM,),
        in_specs=[pl.BlockSpec((TM, D_MODEL), lambda i: (i, 0)),
                  pl.BlockSpec((TM, D_MODEL), lambda i: (i, 0)),
                  pl.BlockSpec((1, 6, D_MODEL), lambda i: (_seg(i, TM), 0, 0)),
                  pl.BlockSpec((1, D_MODEL), lambda i: (0, 0)),
                  pl.BlockSpec((D_MODEL, D_MODEL), lambda i: (0, 0)),
                  pl.BlockSpec((D_MODEL, N_EXPERTS), lambda i: (0, 0)),
                  pl.BlockSpec((1, N_EXPERTS), lambda i: (0, 0))],
        out_specs=[pl.BlockSpec((TM, D_MODEL), lambda i: (i, 0)),
                   pl.BlockSpec((TM * SLAB, LANES), lambda i: (i, 0)),
                   pl.BlockSpec((TM, LANES), lambda i: (i, 0)),
                   pl.BlockSpec((1, N_EXPERTS), lambda i: (0, 0))],
        out_shape=[jax.ShapeDtypeStruct((N_TOK, D_MODEL), F32),
                   jax.ShapeDtypeStruct((N_TOK * SLAB, LANES), F32),
                   jax.ShapeDtypeStruct((N_TOK, LANES), F32),
                   jax.ShapeDtypeStruct((1, N_EXPERTS), F32)],
        scratch_shapes=[pltpu.VMEM((1, N_EXPERTS), F32)],
        compiler_params=_params(("arbitrary",)),
        name="post_router",
    )(mix, x, mod, g2.reshape(1, D_MODEL), w_out, router_w, router_b.reshape(1, N_EXPERTS))


def _route_plan(route, counts):
    eid = route[:, 0:TOP_K].astype(jnp.int32)
    pos = route[:, 2 * TOP_K:3 * TOP_K].astype(jnp.int32)
    cnt = counts[0].astype(jnp.int32)
    blocks = (cnt + TME - 1) // TME
    blk_end = jnp.cumsum(blocks)
    pad_start = (blk_end - blocks) * TME
    experts = jnp.arange(N_EXPERTS, dtype=jnp.int32)
    dest = jnp.sum(jnp.where(eid[:, :, None] == experts, pad_start, 0), axis=-1) + pos
    n_used = blk_end[-1]
    blk = jnp.arange(N_EBLK, dtype=jnp.int32)
    blk_e = jnp.sum((jnp.minimum(blk, n_used - 1)[:, None] >= blk_end[None, :]).astype(jnp.int32), axis=-1)
    blk_e = jnp.minimum(blk_e, N_EXPERTS - 1).astype(jnp.int32)
    is_last = jnp.any((blk[:, None] == blk_end[None, :] - 1) & (blocks[None, :] > 0), axis=-1)
    zero_blk = (is_last | (blk >= n_used)).astype(jnp.int32)
    return dest.astype(jnp.int32), blk_e, n_used.reshape(1).astype(jnp.int32), zero_blk


def _dispatch_kernel(zero_ref, dest_ref, h2_ref, xs_hbm, zbuf, sem, zsem):
    i = pl.program_id(0)

    @pl.when(i == 0)
    def _():
        zbuf[...] = jnp.zeros_like(zbuf)

        def zero_fill(wait):
            def body(j, carry):
                @pl.when(zero_ref[j] != 0)
                def _():
                    cp = pltpu.make_async_copy(
                        zbuf, xs_hbm.at[pl.ds(pl.multiple_of(j * (TME * SLAB), TME * SLAB), TME * SLAB)], zsem)
                    if wait:
                        cp.wait()
                    else:
                        cp.start()
                return carry
            lax.fori_loop(0, N_EBLK, body, 0)

        zero_fill(False)
        zero_fill(True)

    def scatter(wait):
        def body(r, carry):
            for kk in range(TOP_K):
                cp = _slab_copy(h2_ref, xs_hbm, sem, r, dest_ref[0, 0, r * TOP_K + kk])
                if wait:
                    cp.wait()
                else:
                    cp.start(priority=kk % 2)
            return carry
        lax.fori_loop(0, TMC, body, 0, unroll=4)

    scatter(False)
    scatter(True)


def _dispatch(h2, dest3, zero_blk):
    nt = N_TOK // TMC
    grid_spec = pltpu.PrefetchScalarGridSpec(
        num_scalar_prefetch=1,
        grid=(nt,),
        in_specs=[pl.BlockSpec((1, 1, TMC * TOP_K), lambda i, z: (i, 0, 0), memory_space=pltpu.SMEM),
                  pl.BlockSpec((TMC * SLAB, LANES), lambda i, z: (i, 0))],
        out_specs=pl.BlockSpec(memory_space=pl.ANY),
        scratch_shapes=[pltpu.VMEM((TME * SLAB, LANES), F32),
                        pltpu.SemaphoreType.DMA(()),
                        pltpu.SemaphoreType.DMA(())])
    return pl.pallas_call(
        _dispatch_kernel,
        grid_spec=grid_spec,
        out_shape=jax.ShapeDtypeStruct((P_ROWS * SLAB, LANES), F32),
        compiler_params=_params(("arbitrary",)),
        name="moe_dispatch",
    )(zero_blk, dest3, h2)


def _moe_kernel(blk_e_ref, nused_ref, x_ref, win_ref, bin_ref, wout_ref, bout_ref, y_ref, win_bf, wout_bf):
    j = pl.program_id(0)
    n_used = nused_ref[0]

    @pl.when(j < n_used)
    def _():
        changed = jnp.logical_or(j == 0, blk_e_ref[j] != blk_e_ref[jnp.maximum(j - 1, 0)])

        @pl.when(changed)
        def _():
            win_bf[...] = win_ref[0].astype(BF16)
            wout_bf[...] = wout_ref[0].astype(BF16)

        xb = _load_slabs(x_ref, 0, TME).astype(BF16)
        hb = jnp.dot(xb, win_bf[...], preferred_element_type=F32) + bin_ref[0]
        glu = jnp.minimum(hb[:, :D_EXPERT], SWIGLU_LIMIT)
        lin = jnp.clip(hb[:, D_EXPERT:], -SWIGLU_LIMIT, SWIGLU_LIMIT)
        act = glu * jax.nn.sigmoid(SWIGLU_ALPHA * glu) * (lin + 1.0)
        y = jnp.dot(act.astype(BF16), wout_bf[...], preferred_element_type=F32) + bout_ref[0]
        _store_slabs(y_ref, 0, y)

    @pl.when(j >= n_used)
    def _():
        y_ref[...] = jnp.zeros_like(y_ref)


def _moe(xs, blk_e, n_used, w_in, b_in, w_out, b_out, layer):
    e0 = layer * N_EXPERTS
    grid_spec = pltpu.PrefetchScalarGridSpec(
        num_scalar_prefetch=2,
        grid=(N_EBLK,),
        in_specs=[pl.BlockSpec((TME * SLAB, LANES), lambda j, be, nu: (jnp.minimum(j, nu[0] - 1), 0)),
                  pl.BlockSpec((1, D_MODEL, 2 * D_EXPERT), lambda j, be, nu: (e0 + be[j], 0, 0)),
                  pl.BlockSpec((1, 1, 2 * D_EXPERT), lambda j, be, nu: (e0 + be[j], 0, 0)),
                  pl.BlockSpec((1, D_EXPERT, D_MODEL), lambda j, be, nu: (e0 + be[j], 0, 0)),
                  pl.BlockSpec((1, 1, D_MODEL), lambda j, be, nu: (e0 + be[j], 0, 0))],
        out_specs=pl.BlockSpec((TME * SLAB, LANES), lambda j, be, nu: (j, 0)),
        scratch_shapes=[pltpu.VMEM((D_MODEL, 2 * D_EXPERT), BF16),
                        pltpu.VMEM((D_EXPERT, D_MODEL), BF16)])
    return pl.pallas_call(
        _moe_kernel,
        grid_spec=grid_spec,
        out_shape=jax.ShapeDtypeStruct((P_ROWS * SLAB, LANES), F32),
        compiler_params=_params(("arbitrary",)),
        name="moe_ffn",
    )(blk_e, n_used, xs, w_in, b_in, w_out, b_out)


def _combine_kernel(final, dest_ref, destn_ref, y_hbm, xmid_ref, mod_ref, route_ref, fg_ref,
                    o_ref, buf, sem):
    i = pl.program_id(0)
    n = pl.num_programs(0)
    slot = i % 2

    def gather(idx_ref, s, wait):
        def body(r, carry):
            for kk in range(TOP_K):
                cp = _slab_copy(y_hbm, buf, sem.at[s], idx_ref[0, 0, r * TOP_K + kk],
                                (s * TOP_K + kk) * TMC + r)
                if wait:
                    cp.wait()
                else:
                    cp.start(priority=kk % 2)
            return carry
        lax.fori_loop(0, TMC, body, 0, unroll=4)

    @pl.when(i == 0)
    def _():
        gather(dest_ref, 0, False)

    @pl.when(i + 1 < n)
    def _():
        gather(destn_ref, 1 - slot, False)

    gather(dest_ref, slot, True)
    gates = route_ref[...]
    acc = gates[:, TOP_K:TOP_K + 1] * _load_slabs(buf, slot * TOP_K * TMC, TMC)
    for kk in range(1, TOP_K):
        acc = acc + gates[:, TOP_K + kk:TOP_K + kk + 1] * _load_slabs(buf, (slot * TOP_K + kk) * TMC, TMC)
    out = xmid_ref[...] + mod_ref[0][5:6] * acc
    if final:
        out = _rms(out, fg_ref[...])
    o_ref[...] = out


def _combine(y, dest3, xmid, mod, route, final_g, final):
    nt = N_TOK // TMC
    return pl.pallas_call(
        functools.partial(_combine_kernel, final),
        grid=(nt,),
        in_specs=[pl.BlockSpec((1, 1, TMC * TOP_K), lambda i: (i, 0, 0), memory_space=pltpu.SMEM),
                  pl.BlockSpec((1, 1, TMC * TOP_K), lambda i: (jnp.minimum(i + 1, nt - 1), 0, 0),
                               memory_space=pltpu.SMEM),
                  pl.BlockSpec(memory_space=pl.ANY),
                  pl.BlockSpec((TMC, D_MODEL), lambda i: (i, 0)),
                  pl.BlockSpec((1, 6, D_MODEL), lambda i: (_seg(i, TMC), 0, 0)),
                  pl.BlockSpec((TMC, LANES), lambda i: (i, 0)),
                  pl.BlockSpec((1, D_MODEL), lambda i: (0, 0))],
        out_specs=pl.BlockSpec((TMC, D_MODEL), lambda i: (i, 0)),
        out_shape=jax.ShapeDtypeStruct((N_TOK, D_MODEL), F32),
        scratch_shapes=[pltpu.VMEM((2 * TOP_K * TMC * SLAB, LANES), F32),
                        pltpu.SemaphoreType.DMA((2,))],
        compiler_params=_params(("arbitrary",)),
        name="moe_combine_final" if final else "moe_combine",
    )(dest3, dest3, y, xmid, mod, route, final_g.reshape(1, D_MODEL))


def _axial_tables(seq, dim, lanes, offset):
    half = dim // 2
    nf = half // 2
    t = jnp.arange(seq)
    row = (t // GRID_W).astype(F32)
    col = (t % GRID_W).astype(F32)
    inv = ROPE_BASE ** (-jnp.arange(nf, dtype=F32) / nf)
    d = jnp.arange(lanes) - offset
    inside = (d >= 0) & (d < dim)
    dd = jnp.clip(d, 0, dim - 1)
    w = dd % half
    pos = jnp.where((dd // half)[None, :] == 0, row[:, None], col[:, None])
    ang = pos * inv[w % nf][None, :]
    sign = jnp.where(w < nf, -1.0, 1.0)[None, :]
    cos = jnp.where(inside[None, :], jnp.cos(ang), 1.0)
    sins = jnp.where(inside[None, :], sign * jnp.sin(ang), 0.0)
    return cos.astype(F32), sins.astype(F32)


def kernel(x_prompt, x_sample, c, state_ret, state_ssm_re, state_ssm_im, cache_mla_ckv, cache_mla_krope, c_ctx, w_ada, b_ada, norm1_g, norm2_g, final_norm_g, w_in_ab, w_out_ab, ret_decay_logit, ssm_a_re, ssm_a_im, ssm_log_dt, ssm_b_re, ssm_b_im, ssm_c_re, ssm_c_im, ssm_d, ssm_w_glu, w_in_c, mla_q_norm_g, mla_kv_norm_g, mla_w_uq, mla_w_ukv, w_out_c, router_w, router_b, moe_w_in, moe_b_in, moe_w_out, moe_b_out):
    x = jnp.concatenate([x_prompt.reshape(NP, D_MODEL), x_sample.reshape(NS, D_MODEL)], axis=0)
    mods = _modulation(c_ctx, c, w_ada, b_ada)

    rc, rs = _axial_tables(DEC_SEQ, RET_QK_DIM, RET_QK_DIM, 0)
    ret_rope = (jnp.tile(rc, (1, RET_HEADS)), jnp.tile(rs, (1, RET_HEADS)))
    mc, ms = _axial_tables(DEC_SEQ, MLA_ROPE, LANES, MLA_NOPE)
    mla_cos = jnp.concatenate([jnp.ones((TM, LANES), F32), mc], axis=0)
    mla_sin = jnp.concatenate([jnp.zeros((TM, LANES), F32), ms], axis=0)

    w_in_all = moe_w_in.reshape(DEPTH * N_EXPERTS, D_MODEL, 2 * D_EXPERT)
    b_in_all = moe_b_in.reshape(DEPTH * N_EXPERTS, 1, 2 * D_EXPERT)
    w_out_all = moe_w_out.reshape(DEPTH * N_EXPERTS, D_EXPERT, D_MODEL)
    b_out_all = moe_b_out.reshape(DEPTH * N_EXPERTS, 1, D_MODEL)

    ret_l, ssm_re_l, ssm_im_l, ckv_l, kr_l = [], [], [], [], []
    for l in range(DEPTH):
        i = l // 2
        mod = mods[l]
        if l % 2 == 0:
            qkvg, u = _pre_ab(x, norm1_g[l], mod, w_in_ab[i].astype(BF16))
            w1, w2, a16 = _s5_tables(ssm_a_re[i], ssm_a_im[i], ssm_log_dt[i], ssm_b_re[i], ssm_b_im[i],
                                     ssm_c_re[i], ssm_c_im[i])
            x0_p = jnp.zeros((SSM_GROUPS, 2, BATCH, LANES), F32)
            x0_s = jnp.stack(
                [state_ssm_re[:, i].transpose(2, 0, 1, 3).reshape(SSM_GROUPS, DEC_BATCH, LANES),
                 state_ssm_im[:, i].transpose(2, 0, 1, 3).reshape(SSM_GROUPS, DEC_BATCH, LANES)], axis=1)
            yp, xf_p = _s5(u, w1, w2, a16, x0_p, BATCH, SEQ // SSM_CHUNK, 0)
            ys, _ = _s5(u, w1, w2, a16, x0_s, DEC_BATCH, DEC_SEQ // SSM_CHUNK, 1)
            dsum, dec, cd = _ret_tables(ret_decay_logit[i])
            mix0 = jnp.zeros((N_TOK, D_MODEL), BF16)
            mix, s_ret = _ab_core(qkvg, u, yp, dsum, dec, cd, ssm_d[i],
                                  ssm_w_glu[i].astype(BF16), mix0, nb=BATCH, seq=SEQ, row0=0)
            (mix,) = _ab_core(qkvg, u, ys, dsum, dec, cd, ssm_d[i],
                              ssm_w_glu[i].astype(BF16), mix, nb=DEC_BATCH, seq=DEC_SEQ, row0=NP,
                              rope=ret_rope, s0=state_ret[:, i])
            ret_l.append(s_ret)
            xf = xf_p.reshape(SSM_GROUPS, 2, BATCH, 2, SSM_STATE).transpose(1, 2, 3, 0, 4)
            ssm_re_l.append(xf[0])
            ssm_im_l.append(xf[1])
            w_out = w_out_ab[i]
        else:
            w_c = w_in_c[i]
            w_in_pad = jnp.concatenate(
                [w_c[:, :MLA_Q_RANK + MLA_KV_RANK], jnp.zeros((D_MODEL, MLA_NOPE), F32),
                 w_c[:, MLA_Q_RANK + MLA_KV_RANK:], jnp.zeros((D_MODEL, LANES - MLA_NOPE - MLA_ROPE), F32)],
                axis=1).astype(BF16)
            wq = mla_w_uq[i].reshape(MLA_Q_RANK, MLA_HEADS, MLA_NOPE + MLA_ROPE)
            w_uq_pad = jnp.pad(wq, ((0, 0), (0, 0), (0, LANES - MLA_NOPE - MLA_ROPE))).reshape(
                MLA_Q_RANK, MLA_HEADS * LANES).astype(BF16)
            wkv = mla_w_ukv[i].reshape(MLA_KV_RANK, MLA_HEADS, MLA_NOPE + MLA_V)
            wk = jnp.pad(wkv[:, :, :MLA_NOPE], ((0, 0), (0, 0), (0, LANES - MLA_NOPE))).reshape(MLA_KV_RANK, -1)
            wv = wkv[:, :, MLA_NOPE:].reshape(MLA_KV_RANK, -1)
            w_ukv_pad = jnp.concatenate([wk, wv], axis=1).astype(BF16)
            q, ckv, kr, krp = _pre_c(x, norm1_g[l], mod, w_in_pad, mla_q_norm_g[i], mla_kv_norm_g[i],
                                     w_uq_pad, mla_cos, mla_sin)
            ckv_p = ckv[:NP].reshape(BATCH, SEQ, MLA_KV_RANK)
            kr_p = kr[:NP].reshape(BATCH, SEQ, LANES)
            ckv_s = jnp.concatenate([cache_mla_ckv[:, i], ckv[NP:].reshape(DEC_BATCH, DEC_SEQ, MLA_KV_RANK)],
                                    axis=1)
            cache_kr = jnp.pad(cache_mla_krope[:, i], ((0, 0), (0, 0), (MLA_NOPE, LANES - MLA_NOPE - MLA_ROPE)))
            kr_s = jnp.concatenate([cache_kr, krp[NP:].reshape(DEC_BATCH, DEC_SEQ, LANES)], axis=1)
            mix0 = jnp.zeros((N_TOK, D_MODEL), BF16)
            mix = _attention(q, ckv_p, kr_p, w_ukv_pad, mix0, nb=BATCH, seq=SEQ, lk=SEQ, row0=0)
            mix = _attention(q, ckv_s, kr_s, w_ukv_pad, mix, nb=DEC_BATCH, seq=DEC_SEQ,
                             lk=PAST_LEN + DEC_SEQ, row0=NP)
            ckv_l.append(ckv_p)
            kr_l.append(kr_p[:, :, MLA_NOPE:MLA_NOPE + MLA_ROPE])
            w_out = w_out_c[i]
        xmid, h2, route, counts = _post(mix, x, mod, norm2_g[l], w_out.astype(BF16), router_w[l], router_b[l])
        dest, blk_e, n_used, zero_blk = _route_plan(route, counts)
        dest3 = dest.reshape(N_TOK // TMC, 1, TMC * TOP_K)
        xs = _dispatch(h2, dest3, zero_blk)
        y = _moe(xs, blk_e, n_used, w_in_all, b_in_all, w_out_all, b_out_all, l)
        x = _combine(y, dest3, xmid, mod, route, final_norm_g, final=(l == DEPTH - 1))

    y_prompt = x[:NP].reshape(BATCH, SEQ, D_MODEL)
    y_sample = x[NP:].reshape(DEC_BATCH, DEC_SEQ, D_MODEL)
    return (y_prompt, y_sample, jnp.stack(ret_l, axis=1), jnp.stack(ssm_re_l, axis=1),
            jnp.stack(ssm_im_l, axis=1), jnp.stack(ckv_l, axis=1), jnp.stack(kr_l, axis=1))
```

```python
import functools
import math

import jax
import jax.numpy as jnp
from jax import lax
from jax.experimental import pallas as pl
from jax.experimental.pallas import tpu as pltpu

F32 = jnp.float32
BF16 = jnp.bfloat16
HIGHEST = lax.Precision.HIGHEST

D_MODEL = 1024
BATCH = 32
SEQ = 256
DEPTH = 4
DEC_BATCH = 8
DEC_SEQ = 1024
PAST_LEN = 512
GRID_W = 64
RET_HEADS = 4
RET_V_DIM = 128
RET_QK_DIM = 64
RET_CHUNK = 128
SSM_WIDTH = 512
SSM_GROUP = 16
SSM_GROUPS = 32
SSM_STATE = 64
SSM_CHUNK = 16
MLA_HEADS = 8
MLA_NOPE = 64
MLA_ROPE = 32
MLA_V = 128
MLA_Q_RANK = 256
MLA_KV_RANK = 128
N_EXPERTS = 32
TOP_K = 4
D_EXPERT = 1024
SWIGLU_LIMIT = 7.0
SWIGLU_ALPHA = 1.702
ROPE_BASE = 10000.0
NORM_EPS = 1e-6

NP = BATCH * SEQ
NS = DEC_BATCH * DEC_SEQ
N_TOK = NP + NS
N_SEG = 1 + DEC_BATCH
LANES = 128

TM = 512
TMC = 256
TME = 512
N_ASSIGN = N_TOK * TOP_K
N_EBLK = N_ASSIGN // TME + N_EXPERTS
P_ROWS = N_EBLK * TME
VMEM_LIMIT = 56 * 1024 * 1024


def _params(sem, vmem=VMEM_LIMIT):
    return pltpu.CompilerParams(dimension_semantics=sem, vmem_limit_bytes=vmem)


def _bdot(a, b):
    return jnp.dot(a.astype(BF16), b.astype(BF16), preferred_element_type=F32)


def _split(x):
    hi = x.astype(BF16)
    lo = (x - hi.astype(F32)).astype(BF16)
    return hi, lo


def _dot3(a, b):
    ah, al = _split(a)
    bh, bl = _split(b)
    d = functools.partial(jnp.dot, preferred_element_type=F32)
    return d(ah, bh) + d(al, bh) + d(ah, bl)


def _rms(x, g):
    return x * lax.rsqrt(jnp.mean(x * x, axis=-1, keepdims=True) + NORM_EPS) * g


SLAB = D_MODEL // LANES


def _store_slabs(ref, row0, val):
    m = val.shape[0]
    for s in range(SLAB):
        ref[pl.ds(row0 * SLAB + s, m, stride=SLAB), :] = val[:, s * LANES:(s + 1) * LANES]


def _load_slabs(ref, row0, m):
    return jnp.concatenate([ref[pl.ds(row0 * SLAB + s, m, stride=SLAB), :] for s in range(SLAB)], axis=-1)


def _slab_copy(src, dst, sem, src_row, dst_row):
    return pltpu.make_async_copy(src.at[pl.ds(pl.multiple_of(src_row * SLAB, SLAB), SLAB)],
                                 dst.at[pl.ds(pl.multiple_of(dst_row * SLAB, SLAB), SLAB)], sem)


def _seg(i, tm):
    return jnp.where(i < NP // tm, 0, 1 + (i * tm - NP) // DEC_SEQ)


def _mod_kernel(c_ref, w_ref, b_ref, o_ref):
    c = c_ref[...]
    s = c * jax.nn.sigmoid(c)
    o_ref[0] = _dot3(s, w_ref[0]) + b_ref[0]


def _modulation(c_ctx, c, w_ada, b_ada):
    rows = 16
    cvec = jnp.zeros((rows, D_MODEL), F32).at[0].set(c_ctx).at[1:N_SEG].set(c)
    nb = 1536
    out = pl.pallas_call(
        _mod_kernel,
        grid=(DEPTH, 6 * D_MODEL // nb),
        in_specs=[pl.BlockSpec((rows, D_MODEL), lambda l, j: (0, 0)),
                  pl.BlockSpec((1, D_MODEL, nb), lambda l, j: (l, 0, j)),
                  pl.BlockSpec((1, 1, nb), lambda l, j: (l, 0, j))],
        out_specs=pl.BlockSpec((1, rows, nb), lambda l, j: (l, 0, j)),
        out_shape=jax.ShapeDtypeStruct((DEPTH, rows, 6 * D_MODEL), F32),
        compiler_params=_params(("arbitrary", "arbitrary")),
        name="adaln_mod",
    )(cvec, w_ada, b_ada.reshape(DEPTH, 1, 6 * D_MODEL))
    return out[:, :N_SEG].reshape(DEPTH, N_SEG, 6, D_MODEL)


def _stream_in(fused, refs):
    if not fused:
        return refs[0][...], refs[1:], None
    dest_ref, destn_ref, y_hbm, xmid_ref, modp_ref, route_ref = refs[:6]
    buf, sem = refs[-2:]
    x = _combine_tile(TM, dest_ref, destn_ref, y_hbm, xmid_ref, modp_ref, route_ref, buf, sem)
    return x, refs[6:-2], True


def _pre_ab_kernel(fused, *refs):
    x, refs, has_x_out = _stream_in(fused, refs)
    g_ref, mod_ref, w_ref = refs[:3]
    outs = refs[3:]
    if has_x_out:
        outs[0][...] = x
        outs = outs[1:]
    qkvg_ref, u_ref = outs
    m = mod_ref[0]
    h = _rms(x, g_ref[...]) * (1.0 + m[1:2]) + m[0:1]
    p = _bdot(h, w_ref[...])
    qkvg_ref[...] = p[:, :1536]
    for k in range(SSM_WIDTH // LANES):
        u_ref[k] = p[:, 1536 + k * LANES:1536 + (k + 1) * LANES]


def _stream_args(x, prev):
    if prev is None:
        return False, [pl.BlockSpec((TM, D_MODEL), lambda i: (i, 0))], [x], [], [], []
    y, dest, xmid, mod_prev, route = prev
    dest3 = dest.reshape(N_TOK // TM, 1, TM * TOP_K)
    in_specs, scratch = _combine_specs(TM)
    return (True, in_specs, [dest3, dest3, y, xmid, mod_prev, route],
            [pl.BlockSpec((TM, D_MODEL), lambda i: (i, 0))],
            [jax.ShapeDtypeStruct((N_TOK, D_MODEL), F32)], scratch)


def _pre_ab(x, prev, g1, mod, w_in):
    fused, in0, args0, out0, shape0, scratch = _stream_args(x, prev)
    return pl.pallas_call(
        functools.partial(_pre_ab_kernel, fused),
        grid=(N_TOK // TM,),
        in_specs=in0 + [pl.BlockSpec((1, D_MODEL), lambda i: (0, 0)),
                        pl.BlockSpec((1, 6, D_MODEL), lambda i: (_seg(i, TM), 0, 0)),
                        pl.BlockSpec((D_MODEL, 2048), lambda i: (0, 0))],
        out_specs=out0 + [pl.BlockSpec((TM, 1536), lambda i: (i, 0)),
                          pl.BlockSpec((SSM_WIDTH // LANES, TM, LANES), lambda i: (0, i, 0))],
        out_shape=shape0 + [jax.ShapeDtypeStruct((N_TOK, 1536), F32),
                            jax.ShapeDtypeStruct((SSM_WIDTH // LANES, N_TOK, LANES), F32)],
        scratch_shapes=scratch,
        compiler_params=_params(("arbitrary",)),
        name="pre_ab",
    )(*args0, g1.reshape(1, D_MODEL), mod, w_in)


def _rope_lanes(x, cos, sins, first, shift):
    w = x.shape[-1]
    partner = jnp.where(first, pltpu.roll(x, w - shift, 1), pltpu.roll(x, shift, 1))
    return x * cos + partner * sins


def _pre_c_kernel(fused, *refs):
    x, refs, has_x_out = _stream_in(fused, refs)
    g_ref, mod_ref, w_ref, qg_ref, kvg_ref, wuq_ref, cos_ref, sin_ref = refs[:8]
    outs = refs[8:]
    if has_x_out:
        outs[0][...] = x
        outs = outs[1:]
    q_ref, ckv_ref, kr_ref, krp_ref = outs
    m = mod_ref[0]
    h = _rms(x, g_ref[...]) * (1.0 + m[1:2]) + m[0:1]
    p = _bdot(h, w_ref[...])
    cq = _rms(p[:, :MLA_Q_RANK], qg_ref[...])
    ckv_ref[...] = _rms(p[:, MLA_Q_RANK:MLA_Q_RANK + MLA_KV_RANK], kvg_ref[...])
    kr = p[:, MLA_Q_RANK + MLA_KV_RANK:]
    kr_ref[...] = kr
    cos = cos_ref[...]
    sins = sin_ref[...]
    lane = lax.broadcasted_iota(jnp.int32, cos.shape, 1)
    first = (lane % 16) < 8
    krp_ref[...] = _rope_lanes(kr, cos, sins, first, 8)
    q = _bdot(cq, wuq_ref[...])
    for hh in range(MLA_HEADS):
        sl = slice(hh * LANES, (hh + 1) * LANES)
        q_ref[:, sl] = _rope_lanes(q[:, sl], cos, sins, first, 8).astype(BF16)


def _pre_c(x, prev, g1, mod, w_in_pad, q_g, kv_g, w_uq_pad, cos_t, sin_t):
    def tab(i):
        return (jnp.where(i < NP // TM, 0, 1 + ((i * TM - NP) % DEC_SEQ) // TM), 0)
    fused, in0, args0, out0, shape0, scratch = _stream_args(x, prev)
    return pl.pallas_call(
        functools.partial(_pre_c_kernel, fused),
        grid=(N_TOK // TM,),
        in_specs=in0 + [
                  pl.BlockSpec((1, D_MODEL), lambda i: (0, 0)),
                  pl.BlockSpec((1, 6, D_MODEL), lambda i: (_seg(i, TM), 0, 0)),
                  pl.BlockSpec((D_MODEL, 512), lambda i: (0, 0)),
                  pl.BlockSpec((1, MLA_Q_RANK), lambda i: (0, 0)),
                  pl.BlockSpec((1, MLA_KV_RANK), lambda i: (0, 0)),
                  pl.BlockSpec((MLA_Q_RANK, MLA_HEADS * LANES), lambda i: (0, 0)),
                  pl.BlockSpec((TM, LANES), tab),
                  pl.BlockSpec((TM, LANES), tab)],
        out_specs=out0 + [pl.BlockSpec((TM, MLA_HEADS * LANES), lambda i: (i, 0)),
                          pl.BlockSpec((TM, MLA_KV_RANK), lambda i: (i, 0)),
                          pl.BlockSpec((TM, LANES), lambda i: (i, 0)),
                          pl.BlockSpec((TM, LANES), lambda i: (i, 0))],
        out_shape=shape0 + [jax.ShapeDtypeStruct((N_TOK, MLA_HEADS * LANES), BF16),
                            jax.ShapeDtypeStruct((N_TOK, MLA_KV_RANK), F32),
                            jax.ShapeDtypeStruct((N_TOK, LANES), F32),
                            jax.ShapeDtypeStruct((N_TOK, LANES), F32)],
        scratch_shapes=scratch,
        compiler_params=_params(("arbitrary",)),
        name="pre_c",
    )(*args0, g1.reshape(1, D_MODEL), mod, w_in_pad, q_g.reshape(1, -1), kv_g.reshape(1, -1), w_uq_pad,
      cos_t, sin_t)


def _s5_tables(a_re, a_im, log_dt, b_re, b_im, c_re, c_im):
    C = SSM_CHUNK
    a_re = jnp.minimum(a_re.astype(F32), -1e-4)
    a_im = a_im.astype(F32)
    dt = jnp.exp(log_dt.astype(F32))[..., None]
    mag = jnp.exp(dt * a_re)
    ab_re = mag * jnp.cos(dt * a_im)
    ab_im = mag * jnp.sin(dt * a_im)
    den = a_re * a_re + a_im * a_im
    f_re = ((ab_re - 1.0) * a_re + ab_im * a_im) / den
    f_im = (ab_im * a_re - (ab_re - 1.0) * a_im) / den
    b_re = b_re.astype(F32)
    b_im = b_im.astype(F32)
    bb_re = f_re[..., None] * b_re - f_im[..., None] * b_im
    bb_im = f_re[..., None] * b_im + f_im[..., None] * b_re
    ks = jnp.arange(C + 1, dtype=F32)[:, None, None, None]
    pmag = jnp.exp(ks * (dt * a_re)[None])
    p_re = pmag * jnp.cos(ks * (dt * a_im)[None])
    p_im = pmag * jnp.sin(ks * (dt * a_im)[None])
    c_re = c_re.astype(F32)[None]
    c_im = c_im.astype(F32)[None]
    cl_re = c_re * p_re[:, :, :, None, :] - c_im * p_im[:, :, :, None, :]
    cl_im = c_re * p_im[:, :, :, None, :] + c_im * p_re[:, :, :, None, :]
    kmat = (jnp.einsum('kdgpn,dgnq->kdgpq', cl_re[:C], bb_re, precision=HIGHEST)
            - jnp.einsum('kdgpn,dgnq->kdgpq', cl_im[:C], bb_im, precision=HIGHEST))
    def toeplitz(k):
        zero = jnp.zeros_like(k)
        return jnp.stack([jnp.concatenate([zero[:a], k[:C - a]], axis=0) for a in range(C)], axis=0)

    t_f = toeplitz(kmat[:, 0])
    t_b = jnp.swapaxes(toeplitz(kmat[:, 1]), 0, 1)
    m_loc = (t_f + t_b).transpose(2, 0, 4, 1, 3).reshape(SSM_GROUPS, C * SSM_GROUP, C * SSM_GROUP)

    def state_in(d, flip):
        pr = (p_re[:C, d][::-1] if flip else p_re[:C, d])[..., None]
        pi = (p_im[:C, d][::-1] if flip else p_im[:C, d])[..., None]
        er = pr * bb_re[d][None] - pi * bb_im[d][None]
        ei = pr * bb_im[d][None] + pi * bb_re[d][None]
        to = lambda e: e.transpose(1, 0, 3, 2).reshape(SSM_GROUPS, C * SSM_GROUP, SSM_STATE)
        return to(er), to(ei)

    ef_re, ef_im = state_in(0, True)
    eb_re, eb_im = state_in(1, False)
    w1 = jnp.concatenate([m_loc, ef_re, eb_re, ef_im, eb_im], axis=-1)

    def state_out(d, flip):
        to = lambda e: e.transpose(1, 3, 0, 2).reshape(SSM_GROUPS, SSM_STATE, C * SSM_GROUP)
        cr = cl_re[1:C + 1, d][::-1] if flip else cl_re[1:C + 1, d]
        ci = cl_im[1:C + 1, d][::-1] if flip else cl_im[1:C + 1, d]
        return to(cr), to(-ci)

    ff_re, ff_im = state_out(0, False)
    fb_re, fb_im = state_out(1, True)
    w2 = jnp.concatenate([ff_re, fb_re, ff_im, fb_im], axis=1)
    a16 = jnp.stack([jnp.concatenate([p_re[C, 0], p_re[C, 1]], axis=-1),
                     jnp.concatenate([p_im[C, 0], p_im[C, 1]], axis=-1)], axis=1)
    return w1.astype(BF16), w2.astype(BF16), a16


S5_GB = LANES // SSM_GROUP


def _s5_kernel(nb, nc, u_ref, w1_ref, w2_ref, a_ref, x0_ref, y_ref, xf_ref,
               ug_ref, yl_ref, lre_ref, lim_ref, s_ref, l2re_ref, l2im_ref):
    r = nb * nc
    for s in range(SSM_CHUNK):
        piece = u_ref[pl.ds(s, r, stride=SSM_CHUNK), :]
        for gl in range(S5_GB):
            ug_ref[gl, :, s * SSM_GROUP:(s + 1) * SSM_GROUP] = piece[:, gl * SSM_GROUP:(gl + 1) * SSM_GROUP]
    fwd = lax.broadcasted_iota(jnp.int32, (nb, LANES), 1) < SSM_STATE

    def group(gl, carry):
        r1 = _bdot(ug_ref[gl], w1_ref[gl])
        yl_ref[gl] = r1[:, 0:256]
        lre_ref[...] = r1[:, 256:384]
        lim_ref[...] = r1[:, 384:512]
        a = a_ref[gl]
        a_re = a[0:1]
        a_im = a[1:2]
        s_re = x0_ref[gl, 0]
        s_im = x0_ref[gl, 1]
        s_ref[0, 0] = s_re
        s_ref[1, 0] = s_im
        for k in range(nc):
            rf = pl.ds(k, nb, stride=nc)
            rb = pl.ds(nc - 1 - k, nb, stride=nc)
            l_re = jnp.where(fwd, lre_ref[rf, :], lre_ref[rb, :])
            l_im = jnp.where(fwd, lim_ref[rf, :], lim_ref[rb, :])
            n_re = a_re * s_re - a_im * s_im + l_re
            n_im = a_re * s_im + a_im * s_re + l_im
            s_re, s_im = n_re, n_im
            s_ref[0, k + 1] = s_re
            s_ref[1, k + 1] = s_im
        xf_ref[gl, 0] = s_re
        xf_ref[gl, 1] = s_im
        for c in range(nc):
            rows = pl.ds(c, nb, stride=nc)
            l2re_ref[rows, :] = jnp.where(fwd, s_ref[0, c], s_ref[0, nc - 1 - c])
            l2im_ref[rows, :] = jnp.where(fwd, s_ref[1, c], s_ref[1, nc - 1 - c])
        l2 = jnp.concatenate([l2re_ref[...], l2im_ref[...]], axis=-1)
        yl_ref[gl] = yl_ref[gl] + _bdot(l2, w2_ref[gl])
        return carry

    lax.fori_loop(0, S5_GB, group, 0)
    for j in range(SSM_CHUNK):
        tile = jnp.concatenate([yl_ref[gl, :, j * SSM_GROUP:(j + 1) * SSM_GROUP] for gl in range(S5_GB)],
                               axis=-1)
        y_ref[pl.ds(j, r, stride=SSM_CHUNK), :] = tile


def _s5(u4, w1, w2, a16, x0, nb, nc, grp):
    r = nb * nc
    t = r * SSM_CHUNK
    nblk = SSM_WIDTH // LANES
    return pl.pallas_call(
        functools.partial(_s5_kernel, nb, nc),
        grid=(nblk,),
        in_specs=[pl.BlockSpec((None, t, LANES), lambda cb: (cb, grp, 0)),
                  pl.BlockSpec((S5_GB, 256, 512), lambda cb: (cb, 0, 0)),
                  pl.BlockSpec((S5_GB, 256, 256), lambda cb: (cb, 0, 0)),
                  pl.BlockSpec((S5_GB, 2, LANES), lambda cb: (cb, 0, 0)),
                  pl.BlockSpec((S5_GB, 2, nb, LANES), lambda cb: (cb, 0, 0, 0))],
        out_specs=[pl.BlockSpec((None, t, LANES), lambda cb: (cb, 0, 0)),
                   pl.BlockSpec((S5_GB, 2, nb, LANES), lambda cb: (cb, 0, 0, 0))],
        out_shape=[jax.ShapeDtypeStruct((nblk, t, LANES), F32),
                   jax.ShapeDtypeStruct((SSM_GROUPS, 2, nb, LANES), F32)],
        scratch_shapes=[pltpu.VMEM((S5_GB, r, 256), F32),
                        pltpu.VMEM((S5_GB, r, 256), F32),
                        pltpu.VMEM((r, LANES), F32),
                        pltpu.VMEM((r, LANES), F32),
                        pltpu.VMEM((2, nc + 1, nb, LANES), F32),
                        pltpu.VMEM((r, LANES), F32),
                        pltpu.VMEM((r, LANES), F32)],
        compiler_params=_params(("arbitrary",)),
        name="s5_chunked",
    )(u4, w1, w2, a16, x0)


def _ret_tables(logit):
    C = RET_CHUNK
    lg = jax.nn.log_sigmoid(logit.astype(F32))
    pos = jnp.arange(C, dtype=F32)
    diff = pos[:, None] - pos[None, :]
    dsum = (jnp.where(diff >= 0, jnp.exp(jnp.maximum(diff, 0.0)[None] * lg[0][:, None, None]), 0.0)
            + jnp.where(diff <= 0, jnp.exp(jnp.maximum(-diff, 0.0)[None] * lg[1][:, None, None]), 0.0))
    kdf = jnp.exp((C - 1 - pos)[:, None] * lg[0][None, :])
    kdb = jnp.exp(pos[:, None] * lg[1][None, :])
    qdf = jnp.exp((pos + 1)[:, None] * lg[0][None, :])
    qdb = jnp.exp((C - pos)[:, None] * lg[1][None, :])
    dec = jnp.stack([jnp.repeat(t, RET_QK_DIM, axis=1) for t in (kdf, kdb, qdf, qdb)])
    cd = jnp.exp(C * lg)
    return dsum, dec, cd


def _ab_core_kernel(seq, use_pos, with_state, *refs):
    it = iter(refs)
    qkvg_ref, u_ref, ycv_ref, dsum_ref, dec_ref, cd_ref, d_ref, wglu_ref = [next(it) for _ in range(8)]
    if use_pos:
        cos_ref, sin_ref = next(it), next(it)
    s0_ref = next(it) if use_pos else None
    next(it)
    mix_ref = next(it)
    st_ref = next(it) if with_state else None
    C = RET_CHUNK
    n = seq // C
    q = qkvg_ref[:, 0:256] * (RET_QK_DIM ** -0.5)
    k = qkvg_ref[:, 256:512]
    if use_pos:
        cos = cos_ref[...]
        sins = sin_ref[...]
        first = (lax.broadcasted_iota(jnp.int32, cos.shape, 1) % 32) < 16
        q = _rope_lanes(q, cos, sins, first, 16)
        k = _rope_lanes(k, cos, sins, first, 16)
    kdf, kdb, qdf, qdb = dec_ref[0], dec_ref[1], dec_ref[2], dec_ref[3]
    for h in range(RET_HEADS):
        qs = slice(h * RET_QK_DIM, (h + 1) * RET_QK_DIM)
        vs = slice(512 + h * RET_V_DIM, 512 + (h + 1) * RET_V_DIM)
        gs = slice(1024 + h * RET_V_DIM, 1024 + (h + 1) * RET_V_DIM)
        qc, kc, vc, kvf, kvb = [], [], [], [], []
        for c in range(n):
            rows = slice(c * C, (c + 1) * C)
            qc.append(q[rows, qs])
            kc.append(k[rows, qs])
            vc.append(qkvg_ref[rows, vs])
            kvf.append(_bdot((kc[c] * kdf[:, qs]).T, vc[c]))
            kvb.append(_bdot((kc[c] * kdb[:, qs]).T, vc[c]))
        if use_pos:
            sf = s0_ref[0, 0, h]
            sb = s0_ref[0, 1, h]
        else:
            sf = jnp.zeros((RET_QK_DIM, RET_V_DIM), F32)
            sb = jnp.zeros((RET_QK_DIM, RET_V_DIM), F32)
        prev_f = []
        for c in range(n):
            prev_f.append(sf)
            sf = cd_ref[0, h] * sf + kvf[c]
        next_b = [None] * n
        for c in range(n - 1, -1, -1):
            next_b[c] = sb
            sb = cd_ref[1, h] * sb + kvb[c]
        if with_state:
            st_ref[0, 0, h] = sf
            st_ref[0, 1, h] = sb
        for c in range(n):
            rows = slice(c * C, (c + 1) * C)
            sc = lax.dot_general(qc[c].astype(BF16), kc[c].astype(BF16), (((1,), (1,)), ((), ())),
                                 preferred_element_type=F32) * dsum_ref[h]
            o = (_bdot(sc, vc[c]) + _bdot(qc[c] * qdf[:, qs], prev_f[c])
                 + _bdot(qc[c] * qdb[:, qs], next_b[c]))
            o = o * lax.rsqrt(jnp.mean(o * o, axis=-1, keepdims=True) + NORM_EPS)
            g = qkvg_ref[rows, gs]
            mix_ref[rows, h * RET_V_DIM:(h + 1) * RET_V_DIM] = (o * (g * jax.nn.sigmoid(g))).astype(BF16)
    nblk = SSM_WIDTH // LANES
    u = jnp.concatenate([u_ref[kb] for kb in range(nblk)], axis=-1)
    ycv = jnp.concatenate([ycv_ref[kb] for kb in range(nblk)], axis=-1)
    y = ycv + d_ref[...] * u
    z = 0.5 * y * (1.0 + jnp.tanh(math.sqrt(2.0 / math.pi) * (y + 0.044715 * (y * y * y))))
    z = z * jax.nn.sigmoid(_bdot(z, wglu_ref[...]))
    mix_ref[:, 512:1024] = z.astype(BF16)


def _ab_core(qkvg, u, ycv, dsum, dec, cd, d, w_glu, mix_prev, *, nb, seq, row0, rope=None, s0=None):
    use_pos = rope is not None
    with_state = not use_pos
    blk0 = row0 // seq
    in_specs = [pl.BlockSpec((seq, 1536), lambda b: (blk0 + b, 0)),
                pl.BlockSpec((SSM_WIDTH // LANES, seq, LANES), lambda b: (0, blk0 + b, 0)),
                pl.BlockSpec((SSM_WIDTH // LANES, seq, LANES), lambda b: (0, b, 0)),
                pl.BlockSpec((RET_HEADS, RET_CHUNK, RET_CHUNK), lambda b: (0, 0, 0)),
                pl.BlockSpec((4, RET_CHUNK, 256), lambda b: (0, 0, 0)),
                pl.BlockSpec(memory_space=pltpu.SMEM),
                pl.BlockSpec((1, SSM_WIDTH), lambda b: (0, 0)),
                pl.BlockSpec((SSM_WIDTH, SSM_WIDTH), lambda b: (0, 0))]
    args = [qkvg, u, ycv, dsum, dec, cd, d.reshape(1, SSM_WIDTH), w_glu]
    if use_pos:
        in_specs += [pl.BlockSpec((seq, 256), lambda b: (0, 0)),
                     pl.BlockSpec((seq, 256), lambda b: (0, 0)),
                     pl.BlockSpec((1, 2, RET_HEADS, RET_QK_DIM, RET_V_DIM), lambda b: (b, 0, 0, 0, 0))]
        args += [rope[0], rope[1], s0]
    in_specs.append(pl.BlockSpec(memory_space=pl.ANY))
    args.append(mix_prev)
    out_specs = [pl.BlockSpec((seq, D_MODEL), lambda b: (blk0 + b, 0))]
    out_shape = [jax.ShapeDtypeStruct((N_TOK, D_MODEL), BF16)]
    if with_state:
        out_specs.append(pl.BlockSpec((1, 2, RET_HEADS, RET_QK_DIM, RET_V_DIM), lambda b: (b, 0, 0, 0, 0)))
        out_shape.append(jax.ShapeDtypeStruct((nb, 2, RET_HEADS, RET_QK_DIM, RET_V_DIM), F32))
    return pl.pallas_call(
        functools.partial(_ab_core_kernel, seq, use_pos, with_state),
        grid=(nb,),
        in_specs=in_specs,
        out_specs=out_specs,
        out_shape=out_shape,
        input_output_aliases={len(args) - 1: 0},
        compiler_params=_params(("arbitrary",)),
        name="ab_core_pos" if use_pos else "ab_core",
    )(*args)


def _attn_kernel(lk, tq, q_ref, ckv_ref, kr_ref, wukv_ref, _, o_ref, k_sc, v_sc):
    @pl.when(pl.program_id(1) == 0)
    def _():
        kv = _bdot(ckv_ref[0], wukv_ref[...])
        kr = kr_ref[0]
        scale = (MLA_NOPE + MLA_ROPE) ** -0.5
        for h in range(MLA_HEADS):
            sl = slice(h * LANES, (h + 1) * LANES)
            k_sc[:, sl] = ((kv[:, sl] + kr) * scale).astype(BF16)
        v_sc[...] = kv[:, MLA_HEADS * LANES:].astype(BF16)

    for h in range(MLA_HEADS):
        sl = slice(h * LANES, (h + 1) * LANES)
        s = lax.dot_general(q_ref[:, sl], k_sc[:, sl], (((1,), (1,)), ((), ())),
                            preferred_element_type=F32)
        e = jnp.exp(s - jnp.max(s, axis=-1, keepdims=True))
        o = _bdot(e, v_sc[:, sl]) / jnp.sum(e, axis=-1, keepdims=True)
        o_ref[:, sl] = o.astype(BF16)


def _attention(q, ckv_all, kr_all, w_ukv_pad, o_prev, *, nb, seq, lk, row0, tq=256):
    nq = seq // tq
    blk0 = row0 // tq
    return pl.pallas_call(
        functools.partial(_attn_kernel, lk, tq),
        grid=(nb, nq),
        in_specs=[pl.BlockSpec((tq, MLA_HEADS * LANES), lambda b, i: (blk0 + b * nq + i, 0)),
                  pl.BlockSpec((1, lk, MLA_KV_RANK), lambda b, i: (b, 0, 0)),
                  pl.BlockSpec((1, lk, LANES), lambda b, i: (b, 0, 0)),
                  pl.BlockSpec((MLA_KV_RANK, 2 * MLA_HEADS * LANES), lambda b, i: (0, 0)),
                  pl.BlockSpec(memory_space=pl.ANY)],
        out_specs=pl.BlockSpec((tq, MLA_HEADS * MLA_V), lambda b, i: (blk0 + b * nq + i, 0)),
        out_shape=jax.ShapeDtypeStruct((N_TOK, MLA_HEADS * MLA_V), BF16),
        scratch_shapes=[pltpu.VMEM((lk, MLA_HEADS * LANES), BF16),
                        pltpu.VMEM((lk, MLA_HEADS * MLA_V), BF16)],
        input_output_aliases={4: 0},
        compiler_params=_params(("arbitrary", "arbitrary")),
        name="mla_attn_%d" % lk,
    )(q, ckv_all, kr_all, w_ukv_pad, o_prev)


def _post_kernel(mix_ref, x_ref, mod_ref, g_ref, wo_ref, rw_ref, rb_ref,
                 xmid_ref, h2_ref, route_ref, cnt_ref, base_ref):
    i = pl.program_id(0)

    @pl.when(i == 0)
    def _():
        base_ref[...] = jnp.zeros_like(base_ref)

    m = mod_ref[0]
    xm = x_ref[...] + m[2:3] * jnp.dot(mix_ref[...], wo_ref[...], preferred_element_type=F32)
    xmid_ref[...] = xm
    h2 = _rms(xm, g_ref[...]) * (1.0 + m[4:5]) + m[3:4]
    _store_slabs(h2_ref, 0, h2)
    logits = _dot3(h2, rw_ref[...]) + rb_ref[...]
    lane_f = lax.broadcasted_iota(jnp.int32, logits.shape, 1).astype(F32)
    l = logits
    vals, sels = [], []
    for _k in range(TOP_K):
        mx = jnp.max(l, axis=-1, keepdims=True)
        idx = jnp.min(jnp.where(l == mx, lane_f, float(N_EXPERTS)), axis=-1, keepdims=True)
        sel = lane_f == idx
        vals.append(mx)
        sels.append(sel)
        l = jnp.where(sel, -jnp.inf, l)
    ex = [jnp.exp(v - vals[0]) for v in vals]
    den = ex[0] + ex[1] + ex[2] + ex[3]
    onehot = (sels[0] | sels[1] | sels[2] | sels[3]).astype(F32)
    r_i = lax.broadcasted_iota(jnp.int32, (TM, TM), 0)
    c_i = lax.broadcasted_iota(jnp.int32, (TM, TM), 1)
    tri = (c_i < r_i).astype(BF16)
    before = base_ref[...] + jnp.dot(tri, onehot.astype(BF16), preferred_element_type=F32)
    out_lane = lax.broadcasted_iota(jnp.int32, (TM, LANES), 1)
    route = jnp.zeros((TM, LANES), F32)
    for kk in range(TOP_K):
        sel_f = sels[kk].astype(F32)
        eid = jnp.sum(sel_f * lane_f, axis=-1, keepdims=True)
        pos = jnp.sum(sel_f * before, axis=-1, keepdims=True)
        route = jnp.where(out_lane == kk, eid, route)
        route = jnp.where(out_lane == TOP_K + kk, ex[kk] / den, route)
        route = jnp.where(out_lane == 2 * TOP_K + kk, pos, route)
    route_ref[...] = route
    base_ref[...] = base_ref[...] + jnp.sum(onehot, axis=0, keepdims=True)
    cnt_ref[...] = base_ref[...]


def _post(mix, x, mod, g2, w_out, router_w, router_b):
    return pl.pallas_call(
        _post_kernel,
        grid=(N_TOK // TM,),
        in_specs=[pl.BlockSpec((TM, D_MODEL), lambda i: (i, 0)),
                  pl.BlockSpec((TM, D_MODEL), lambda i: (i, 0)),
                  pl.BlockSpec((1, 6, D_MODEL), lambda i: (_seg(i, TM), 0, 0)),
                  pl.BlockSpec((1, D_MODEL), lambda i: (0, 0)),
                  pl.BlockSpec((D_MODEL, D_MODEL), lambda i: (0, 0)),
                  pl.BlockSpec((D_MODEL, N_EXPERTS), lambda i: (0, 0)),
                  pl.BlockSpec((1, N_EXPERTS), lambda i: (0, 0))],
        out_specs=[pl.BlockSpec((TM, D_MODEL), lambda i: (i, 0)),
                   pl.BlockSpec((TM * SLAB, LANES), lambda i: (i, 0)),
                   pl.BlockSpec((TM, LANES), lambda i: (i, 0)),
                   pl.BlockSpec((1, N_EXPERTS), lambda i: (0, 0))],
        out_shape=[jax.ShapeDtypeStruct((N_TOK, D_MODEL), F32),
                   jax.ShapeDtypeStruct((N_TOK * SLAB, LANES), F32),
                   jax.ShapeDtypeStruct((N_TOK, LANES), F32),
                   jax.ShapeDtypeStruct((1, N_EXPERTS), F32)],
        scratch_shapes=[pltpu.VMEM((1, N_EXPERTS), F32)],
        compiler_params=_params(("arbitrary",)),
        name="post_router",
    )(mix, x, mod, g2.reshape(1, D_MODEL), w_out, router_w, router_b.reshape(1, N_EXPERTS))


def _route_plan(route, counts):
    eid = route[:, 0:TOP_K].astype(jnp.int32)
    pos = route[:, 2 * TOP_K:3 * TOP_K].astype(jnp.int32)
    cnt = counts[0].astype(jnp.int32)
    blocks = (cnt + TME - 1) // TME
    blk_end = jnp.cumsum(blocks)
    pad_start = (blk_end - blocks) * TME
    experts = jnp.arange(N_EXPERTS, dtype=jnp.int32)
    dest = jnp.sum(jnp.where(eid[:, :, None] == experts, pad_start, 0), axis=-1) + pos
    n_used = blk_end[-1]
    blk = jnp.arange(N_EBLK, dtype=jnp.int32)
    blk_e = jnp.sum((jnp.minimum(blk, n_used - 1)[:, None] >= blk_end[None, :]).astype(jnp.int32), axis=-1)
    blk_e = jnp.minimum(blk_e, N_EXPERTS - 1).astype(jnp.int32)
    is_last = jnp.any((blk[:, None] == blk_end[None, :] - 1) & (blocks[None, :] > 0), axis=-1)
    zero_blk = (is_last | (blk >= n_used)).astype(jnp.int32)
    return dest.astype(jnp.int32), blk_e, n_used.reshape(1).astype(jnp.int32), zero_blk


def _dispatch_kernel(zero_ref, dest_ref, h2_ref, xs_hbm, zbuf, sem, zsem):
    i = pl.program_id(0)

    @pl.when(i == 0)
    def _():
        zbuf[...] = jnp.zeros_like(zbuf)

        def zero_fill(wait):
            def body(j, carry):
                @pl.when(zero_ref[j] != 0)
                def _():
                    cp = pltpu.make_async_copy(
                        zbuf, xs_hbm.at[pl.ds(pl.multiple_of(j * (TME * SLAB), TME * SLAB), TME * SLAB)], zsem)
                    if wait:
                        cp.wait()
                    else:
                        cp.start()
                return carry
            lax.fori_loop(0, N_EBLK, body, 0)

        zero_fill(False)
        zero_fill(True)

    def scatter(wait):
        def body(r, carry):
            for kk in range(TOP_K):
                cp = _slab_copy(h2_ref, xs_hbm, sem, r, dest_ref[0, 0, r * TOP_K + kk])
                if wait:
                    cp.wait()
                else:
                    cp.start(priority=kk % 2)
            return carry
        lax.fori_loop(0, TMC, body, 0, unroll=4)

    scatter(False)
    scatter(True)


def _dispatch(h2, dest3, zero_blk):
    nt = N_TOK // TMC
    grid_spec = pltpu.PrefetchScalarGridSpec(
        num_scalar_prefetch=1,
        grid=(nt,),
        in_specs=[pl.BlockSpec((1, 1, TMC * TOP_K), lambda i, z: (i, 0, 0), memory_space=pltpu.SMEM),
                  pl.BlockSpec((TMC * SLAB, LANES), lambda i, z: (i, 0))],
        out_specs=pl.BlockSpec(memory_space=pl.ANY),
        scratch_shapes=[pltpu.VMEM((TME * SLAB, LANES), F32),
                        pltpu.SemaphoreType.DMA(()),
                        pltpu.SemaphoreType.DMA(())])
    return pl.pallas_call(
        _dispatch_kernel,
        grid_spec=grid_spec,
        out_shape=jax.ShapeDtypeStruct((P_ROWS * SLAB, LANES), F32),
        compiler_params=_params(("arbitrary",)),
        name="moe_dispatch",
    )(zero_blk, dest3, h2)


def _moe_kernel(blk_e_ref, nused_ref, x_ref, win_ref, bin_ref, wout_ref, bout_ref, y_ref, win_bf, wout_bf):
    j = pl.program_id(0)
    n_used = nused_ref[0]

    @pl.when(j < n_used)
    def _():
        changed = jnp.logical_or(j == 0, blk_e_ref[j] != blk_e_ref[jnp.maximum(j - 1, 0)])

        @pl.when(changed)
        def _():
            win_bf[...] = win_ref[0].astype(BF16)
            wout_bf[...] = wout_ref[0].astype(BF16)

        xb = _load_slabs(x_ref, 0, TME).astype(BF16)
        hb = jnp.dot(xb, win_bf[...], preferred_element_type=F32) + bin_ref[0]
        glu = jnp.minimum(hb[:, :D_EXPERT], SWIGLU_LIMIT)
        lin = jnp.clip(hb[:, D_EXPERT:], -SWIGLU_LIMIT, SWIGLU_LIMIT)
        act = glu * jax.nn.sigmoid(SWIGLU_ALPHA * glu) * (lin + 1.0)
        y = jnp.dot(act.astype(BF16), wout_bf[...], preferred_element_type=F32) + bout_ref[0]
        _store_slabs(y_ref, 0, y)

    @pl.when(j >= n_used)
    def _():
        y_ref[...] = jnp.zeros_like(y_ref)


def _moe(xs, blk_e, n_used, w_in, b_in, w_out, b_out, layer):
    e0 = layer * N_EXPERTS
    grid_spec = pltpu.PrefetchScalarGridSpec(
        num_scalar_prefetch=2,
        grid=(N_EBLK,),
        in_specs=[pl.BlockSpec((TME * SLAB, LANES), lambda j, be, nu: (jnp.minimum(j, nu[0] - 1), 0)),
                  pl.BlockSpec((1, D_MODEL, 2 * D_EXPERT), lambda j, be, nu: (e0 + be[j], 0, 0)),
                  pl.BlockSpec((1, 1, 2 * D_EXPERT), lambda j, be, nu: (e0 + be[j], 0, 0)),
                  pl.BlockSpec((1, D_EXPERT, D_MODEL), lambda j, be, nu: (e0 + be[j], 0, 0)),
                  pl.BlockSpec((1, 1, D_MODEL), lambda j, be, nu: (e0 + be[j], 0, 0))],
        out_specs=pl.BlockSpec((TME * SLAB, LANES), lambda j, be, nu: (j, 0)),
        scratch_shapes=[pltpu.VMEM((D_MODEL, 2 * D_EXPERT), BF16),
                        pltpu.VMEM((D_EXPERT, D_MODEL), BF16)])
    return pl.pallas_call(
        _moe_kernel,
        grid_spec=grid_spec,
        out_shape=jax.ShapeDtypeStruct((P_ROWS * SLAB, LANES), F32),
        compiler_params=_params(("arbitrary",)),
        name="moe_ffn",
    )(blk_e, n_used, xs, w_in, b_in, w_out, b_out)


def _combine_tile(tm, dest_ref, destn_ref, y_hbm, xmid_ref, mod_ref, route_ref, buf, sem):
    i = pl.program_id(0)
    n = pl.num_programs(0)
    slot = i % 2

    def gather(idx_ref, s, wait):
        def body(r, carry):
            for kk in range(TOP_K):
                cp = _slab_copy(y_hbm, buf, sem.at[s], idx_ref[0, 0, r * TOP_K + kk],
                                (s * TOP_K + kk) * tm + r)
                if wait:
                    cp.wait()
                else:
                    cp.start(priority=kk % 2)
            return carry
        lax.fori_loop(0, tm, body, 0, unroll=4)

    @pl.when(i == 0)
    def _():
        gather(dest_ref, 0, False)

    @pl.when(i + 1 < n)
    def _():
        gather(destn_ref, 1 - slot, False)

    gather(dest_ref, slot, True)
    gates = route_ref[...]
    acc = gates[:, TOP_K:TOP_K + 1] * _load_slabs(buf, slot * TOP_K * tm, tm)
    for kk in range(1, TOP_K):
        acc = acc + gates[:, TOP_K + kk:TOP_K + kk + 1] * _load_slabs(buf, (slot * TOP_K + kk) * tm, tm)
    return xmid_ref[...] + mod_ref[0][5:6] * acc


def _combine_specs(tm):
    nt = N_TOK // tm
    in_specs = [pl.BlockSpec((1, 1, tm * TOP_K), lambda i: (i, 0, 0), memory_space=pltpu.SMEM),
                pl.BlockSpec((1, 1, tm * TOP_K), lambda i: (jnp.minimum(i + 1, nt - 1), 0, 0),
                             memory_space=pltpu.SMEM),
                pl.BlockSpec(memory_space=pl.ANY),
                pl.BlockSpec((tm, D_MODEL), lambda i: (i, 0)),
                pl.BlockSpec((1, 6, D_MODEL), lambda i: (_seg(i, tm), 0, 0)),
                pl.BlockSpec((tm, LANES), lambda i: (i, 0))]
    scratch = [pltpu.VMEM((2 * TOP_K * tm * SLAB, LANES), F32), pltpu.SemaphoreType.DMA((2,))]
    return in_specs, scratch


def _combine_kernel(dest_ref, destn_ref, y_hbm, xmid_ref, mod_ref, route_ref, fg_ref, o_ref, buf, sem):
    out = _combine_tile(TMC, dest_ref, destn_ref, y_hbm, xmid_ref, mod_ref, route_ref, buf, sem)
    o_ref[...] = _rms(out, fg_ref[...])


def _combine_final(y, dest, xmid, mod, route, final_g):
    dest3 = dest.reshape(N_TOK // TMC, 1, TMC * TOP_K)
    in_specs, scratch = _combine_specs(TMC)
    return pl.pallas_call(
        _combine_kernel,
        grid=(N_TOK // TMC,),
        in_specs=in_specs + [pl.BlockSpec((1, D_MODEL), lambda i: (0, 0))],
        out_specs=pl.BlockSpec((TMC, D_MODEL), lambda i: (i, 0)),
        out_shape=jax.ShapeDtypeStruct((N_TOK, D_MODEL), F32),
        scratch_shapes=scratch,
        compiler_params=_params(("arbitrary",)),
        name="moe_combine_final",
    )(dest3, dest3, y, xmid, mod, route, final_g.reshape(1, D_MODEL))


def _axial_tables(seq, dim, lanes, offset):
    half = dim // 2
    nf = half // 2
    t = jnp.arange(seq)
    row = (t // GRID_W).astype(F32)
    col = (t % GRID_W).astype(F32)
    inv = ROPE_BASE ** (-jnp.arange(nf, dtype=F32) / nf)
    d = jnp.arange(lanes) - offset
    inside = (d >= 0) & (d < dim)
    dd = jnp.clip(d, 0, dim - 1)
    w = dd % half
    pos = jnp.where((dd // half)[None, :] == 0, row[:, None], col[:, None])
    ang = pos * inv[w % nf][None, :]
    sign = jnp.where(w < nf, -1.0, 1.0)[None, :]
    cos = jnp.where(inside[None, :], jnp.cos(ang), 1.0)
    sins = jnp.where(inside[None, :], sign * jnp.sin(ang), 0.0)
    return cos.astype(F32), sins.astype(F32)


def kernel(x_prompt, x_sample, c, state_ret, state_ssm_re, state_ssm_im, cache_mla_ckv, cache_mla_krope, c_ctx, w_ada, b_ada, norm1_g, norm2_g, final_norm_g, w_in_ab, w_out_ab, ret_decay_logit, ssm_a_re, ssm_a_im, ssm_log_dt, ssm_b_re, ssm_b_im, ssm_c_re, ssm_c_im, ssm_d, ssm_w_glu, w_in_c, mla_q_norm_g, mla_kv_norm_g, mla_w_uq, mla_w_ukv, w_out_c, router_w, router_b, moe_w_in, moe_b_in, moe_w_out, moe_b_out):
    x = jnp.concatenate([x_prompt.reshape(NP, D_MODEL), x_sample.reshape(NS, D_MODEL)], axis=0)
    mods = _modulation(c_ctx, c, w_ada, b_ada)

    rc, rs = _axial_tables(DEC_SEQ, RET_QK_DIM, RET_QK_DIM, 0)
    ret_rope = (jnp.tile(rc, (1, RET_HEADS)), jnp.tile(rs, (1, RET_HEADS)))
    mc, ms = _axial_tables(DEC_SEQ, MLA_ROPE, LANES, MLA_NOPE)
    mla_cos = jnp.concatenate([jnp.ones((TM, LANES), F32), mc], axis=0)
    mla_sin = jnp.concatenate([jnp.zeros((TM, LANES), F32), ms], axis=0)

    w_in_all = moe_w_in.reshape(DEPTH * N_EXPERTS, D_MODEL, 2 * D_EXPERT)
    b_in_all = moe_b_in.reshape(DEPTH * N_EXPERTS, 1, 2 * D_EXPERT)
    w_out_all = moe_w_out.reshape(DEPTH * N_EXPERTS, D_EXPERT, D_MODEL)
    b_out_all = moe_b_out.reshape(DEPTH * N_EXPERTS, 1, D_MODEL)

    ret_l, ssm_re_l, ssm_im_l, ckv_l, kr_l = [], [], [], [], []
    prev = None
    mix = jnp.zeros((N_TOK, D_MODEL), BF16)
    for l in range(DEPTH):
        i = l // 2
        mod = mods[l]
        if l % 2 == 0:
            outs = _pre_ab(x, prev, norm1_g[l], mod, w_in_ab[i].astype(BF16))
            if prev is not None:
                x = outs[0]
            qkvg, u = outs[-2:]
            w1, w2, a16 = _s5_tables(ssm_a_re[i], ssm_a_im[i], ssm_log_dt[i], ssm_b_re[i], ssm_b_im[i],
                                     ssm_c_re[i], ssm_c_im[i])
            x0_p = jnp.zeros((SSM_GROUPS, 2, BATCH, LANES), F32)
            x0_s = jnp.stack(
                [state_ssm_re[:, i].transpose(2, 0, 1, 3).reshape(SSM_GROUPS, DEC_BATCH, LANES),
                 state_ssm_im[:, i].transpose(2, 0, 1, 3).reshape(SSM_GROUPS, DEC_BATCH, LANES)], axis=1)
            yp, xf_p = _s5(u, w1, w2, a16, x0_p, BATCH, SEQ // SSM_CHUNK, 0)
            ys, _ = _s5(u, w1, w2, a16, x0_s, DEC_BATCH, DEC_SEQ // SSM_CHUNK, 1)
            dsum, dec, cd = _ret_tables(ret_decay_logit[i])
            mix, s_ret = _ab_core(qkvg, u, yp, dsum, dec, cd, ssm_d[i],
                                  ssm_w_glu[i].astype(BF16), mix, nb=BATCH, seq=SEQ, row0=0)
            (mix,) = _ab_core(qkvg, u, ys, dsum, dec, cd, ssm_d[i],
                              ssm_w_glu[i].astype(BF16), mix, nb=DEC_BATCH, seq=DEC_SEQ, row0=NP,
                              rope=ret_rope, s0=state_ret[:, i])
            ret_l.append(s_ret)
            xf = xf_p.reshape(SSM_GROUPS, 2, BATCH, 2, SSM_STATE).transpose(1, 2, 3, 0, 4)
            ssm_re_l.append(xf[0])
            ssm_im_l.append(xf[1])
            w_out = w_out_ab[i]
        else:
            w_c = w_in_c[i]
            w_in_pad = jnp.concatenate(
                [w_c[:, :MLA_Q_RANK + MLA_KV_RANK], jnp.zeros((D_MODEL, MLA_NOPE), F32),
                 w_c[:, MLA_Q_RANK + MLA_KV_RANK:], jnp.zeros((D_MODEL, LANES - MLA_NOPE - MLA_ROPE), F32)],
                axis=1).astype(BF16)
            wq = mla_w_uq[i].reshape(MLA_Q_RANK, MLA_HEADS, MLA_NOPE + MLA_ROPE)
            w_uq_pad = jnp.pad(wq, ((0, 0), (0, 0), (0, LANES - MLA_NOPE - MLA_ROPE))).reshape(
                MLA_Q_RANK, MLA_HEADS * LANES).astype(BF16)
            wkv = mla_w_ukv[i].reshape(MLA_KV_RANK, MLA_HEADS, MLA_NOPE + MLA_V)
            wk = jnp.pad(wkv[:, :, :MLA_NOPE], ((0, 0), (0, 0), (0, LANES - MLA_NOPE))).reshape(MLA_KV_RANK, -1)
            wv = wkv[:, :, MLA_NOPE:].reshape(MLA_KV_RANK, -1)
            w_ukv_pad = jnp.concatenate([wk, wv], axis=1).astype(BF16)
            outs = _pre_c(x, prev, norm1_g[l], mod, w_in_pad, mla_q_norm_g[i], mla_kv_norm_g[i],
                          w_uq_pad, mla_cos, mla_sin)
            if prev is not None:
                x = outs[0]
            q, ckv, kr, krp = outs[-4:]
            ckv_p = ckv[:NP].reshape(BATCH, SEQ, MLA_KV_RANK)
            kr_p = kr[:NP].reshape(BATCH, SEQ, LANES)
            ckv_s = jnp.concatenate([cache_mla_ckv[:, i], ckv[NP:].reshape(DEC_BATCH, DEC_SEQ, MLA_KV_RANK)],
                                    axis=1)
            cache_kr = jnp.pad(cache_mla_krope[:, i], ((0, 0), (0, 0), (MLA_NOPE, LANES - MLA_NOPE - MLA_ROPE)))
            kr_s = jnp.concatenate([cache_kr, krp[NP:].reshape(DEC_BATCH, DEC_SEQ, LANES)], axis=1)
            mix = _attention(q, ckv_p, kr_p, w_ukv_pad, mix, nb=BATCH, seq=SEQ, lk=SEQ, row0=0)
            mix = _attention(q, ckv_s, kr_s, w_ukv_pad, mix, nb=DEC_BATCH, seq=DEC_SEQ,
                             lk=PAST_LEN + DEC_SEQ, row0=NP)
            ckv_l.append(ckv_p)
            kr_l.append(kr_p[:, :, MLA_NOPE:MLA_NOPE + MLA_ROPE])
            w_out = w_out_c[i]
        xmid, h2, route, counts = _post(mix, x, mod, norm2_g[l], w_out.astype(BF16), router_w[l], router_b[l])
        dest, blk_e, n_used, zero_blk = _route_plan(route, counts)
        dest3 = dest.reshape(N_TOK // TMC, 1, TMC * TOP_K)
        xs = _dispatch(h2, dest3, zero_blk)
        y = _moe(xs, blk_e, n_used, w_in_all, b_in_all, w_out_all, b_out_all, l)
        prev = (y, dest, xmid, mod, route)

    x = _combine_final(*prev, final_norm_g)

    y_prompt = x[:NP].reshape(BATCH, SEQ, D_MODEL)
    y_sample = x[NP:].reshape(DEC_BATCH, DEC_SEQ, D_MODEL)
    return (y_prompt, y_sample, jnp.stack(ret_l, axis=1), jnp.stack(ssm_re_l, axis=1),
            jnp.stack(ssm_im_l, axis=1), jnp.stack(ckv_l, axis=1), jnp.stack(kr_l, axis=1))
```

```python
import functools
import math

import jax
import jax.numpy as jnp
import numpy as np
from jax import lax
from jax.experimental import pallas as pl
from jax.experimental.pallas import tpu as pltpu

F32 = jnp.float32
BF16 = jnp.bfloat16
HIGHEST = lax.Precision.HIGHEST

D_MODEL = 1024
BATCH = 32
SEQ = 256
DEPTH = 4
DEC_BATCH = 8
DEC_SEQ = 1024
PAST_LEN = 512
GRID_W = 64
RET_HEADS = 4
RET_V_DIM = 128
RET_QK_DIM = 64
RET_CHUNK = 128
SSM_WIDTH = 512
SSM_GROUP = 16
SSM_GROUPS = 32
SSM_STATE = 64
SSM_CHUNK = 16
MLA_HEADS = 8
MLA_NOPE = 64
MLA_ROPE = 32
MLA_V = 128
MLA_Q_RANK = 256
MLA_KV_RANK = 128
N_EXPERTS = 32
TOP_K = 4
D_EXPERT = 1024
SWIGLU_LIMIT = 7.0
SWIGLU_ALPHA = 1.702
ROPE_BASE = 10000.0
NORM_EPS = 1e-6

NP = BATCH * SEQ
NS = DEC_BATCH * DEC_SEQ
N_TOK = NP + NS
N_SEG = 1 + DEC_BATCH
LANES = 128

TM = 512
TMC = 256
TME = 512
N_ASSIGN = N_TOK * TOP_K
N_EBLK = N_ASSIGN // TME + N_EXPERTS
P_ROWS = N_EBLK * TME
VMEM_LIMIT = 56 * 1024 * 1024


def _params(sem, vmem=VMEM_LIMIT):
    return pltpu.CompilerParams(dimension_semantics=sem, vmem_limit_bytes=vmem)


def _bdot(a, b):
    return jnp.dot(a.astype(BF16), b.astype(BF16), preferred_element_type=F32)


def _split(x):
    hi = x.astype(BF16)
    lo = (x - hi.astype(F32)).astype(BF16)
    return hi, lo


def _dot3(a, b):
    ah, al = _split(a)
    bh, bl = _split(b)
    d = functools.partial(jnp.dot, preferred_element_type=F32)
    return d(ah, bh) + d(al, bh) + d(ah, bl)


def _rms(x, g):
    return x * lax.rsqrt(jnp.mean(x * x, axis=-1, keepdims=True) + NORM_EPS) * g


SLAB = D_MODEL // LANES


def _store_slabs(ref, row0, val):
    m = val.shape[0]
    for s in range(SLAB):
        ref[pl.ds(row0 * SLAB + s, m, stride=SLAB), :] = val[:, s * LANES:(s + 1) * LANES]


def _load_slabs(ref, row0, m):
    return jnp.concatenate([ref[pl.ds(row0 * SLAB + s, m, stride=SLAB), :] for s in range(SLAB)], axis=-1)


def _slab_copy(src, dst, sem, src_row, dst_row):
    return pltpu.make_async_copy(src.at[pl.ds(pl.multiple_of(src_row * SLAB, SLAB), SLAB)],
                                 dst.at[pl.ds(pl.multiple_of(dst_row * SLAB, SLAB), SLAB)], sem)


def _seg(i, tm):
    return jnp.where(i < NP // tm, 0, 1 + (i * tm - NP) // DEC_SEQ)


def _mod_kernel(c_ref, w_ref, b_ref, o_ref):
    c = c_ref[...]
    s = c * jax.nn.sigmoid(c)
    o_ref[0] = _dot3(s, w_ref[0]) + b_ref[0]


def _modulation(c_ctx, c, w_ada, b_ada):
    rows = 16
    cvec = jnp.zeros((rows, D_MODEL), F32).at[0].set(c_ctx).at[1:N_SEG].set(c)
    nb = 1536
    out = pl.pallas_call(
        _mod_kernel,
        grid=(DEPTH, 6 * D_MODEL // nb),
        in_specs=[pl.BlockSpec((rows, D_MODEL), lambda l, j: (0, 0)),
                  pl.BlockSpec((1, D_MODEL, nb), lambda l, j: (l, 0, j)),
                  pl.BlockSpec((1, 1, nb), lambda l, j: (l, 0, j))],
        out_specs=pl.BlockSpec((1, rows, nb), lambda l, j: (l, 0, j)),
        out_shape=jax.ShapeDtypeStruct((DEPTH, rows, 6 * D_MODEL), F32),
        compiler_params=_params(("arbitrary", "arbitrary")),
        name="adaln_mod",
    )(cvec, w_ada, b_ada.reshape(DEPTH, 1, 6 * D_MODEL))
    return out[:, :N_SEG].reshape(DEPTH, N_SEG, 6, D_MODEL)


def _pre_ab_kernel(x_ref, g_ref, mod_ref, w_ref, qkvg_ref, u_ref):
    m = mod_ref[0]
    h = _rms(x_ref[...], g_ref[...]) * (1.0 + m[1:2]) + m[0:1]
    p = _bdot(h, w_ref[...])
    qkvg_ref[...] = p[:, :1536]
    for k in range(SSM_WIDTH // LANES):
        u_ref[k] = p[:, 1536 + k * LANES:1536 + (k + 1) * LANES]


def _pre_ab(x, g1, mod, w_in):
    return pl.pallas_call(
        _pre_ab_kernel,
        grid=(N_TOK // TM,),
        in_specs=[pl.BlockSpec((TM, D_MODEL), lambda i: (i, 0)),
                  pl.BlockSpec((1, D_MODEL), lambda i: (0, 0)),
                  pl.BlockSpec((1, 6, D_MODEL), lambda i: (_seg(i, TM), 0, 0)),
                  pl.BlockSpec((D_MODEL, 2048), lambda i: (0, 0))],
        out_specs=[pl.BlockSpec((TM, 1536), lambda i: (i, 0)),
                   pl.BlockSpec((SSM_WIDTH // LANES, TM, LANES), lambda i: (0, i, 0))],
        out_shape=[jax.ShapeDtypeStruct((N_TOK, 1536), F32),
                   jax.ShapeDtypeStruct((SSM_WIDTH // LANES, N_TOK, LANES), F32)],
        compiler_params=_params(("arbitrary",)),
        name="pre_ab",
    )(x, g1.reshape(1, D_MODEL), mod, w_in)


def _rope_lanes(x, cos, sins, first, shift):
    w = x.shape[-1]
    partner = jnp.where(first, pltpu.roll(x, w - shift, 1), pltpu.roll(x, shift, 1))
    return x * cos + partner * sins


def _pre_c_kernel(x_ref, g_ref, mod_ref, w_ref, qg_ref, kvg_ref, wuq_ref, cos_ref, sin_ref,
                  q_ref, ckv_ref, kr_ref, krp_ref):
    m = mod_ref[0]
    h = _rms(x_ref[...], g_ref[...]) * (1.0 + m[1:2]) + m[0:1]
    p = _bdot(h, w_ref[...])
    cq = _rms(p[:, :MLA_Q_RANK], qg_ref[...])
    ckv_ref[...] = _rms(p[:, MLA_Q_RANK:MLA_Q_RANK + MLA_KV_RANK], kvg_ref[...])
    kr = p[:, MLA_Q_RANK + MLA_KV_RANK:]
    kr_ref[...] = kr
    cos = cos_ref[...]
    sins = sin_ref[...]
    lane = lax.broadcasted_iota(jnp.int32, cos.shape, 1)
    first = (lane % 16) < 8
    krp_ref[...] = _rope_lanes(kr, cos, sins, first, 8)
    q = _bdot(cq, wuq_ref[...])
    for hh in range(MLA_HEADS):
        sl = slice(hh * LANES, (hh + 1) * LANES)
        q_ref[:, sl] = _rope_lanes(q[:, sl], cos, sins, first, 8).astype(BF16)


def _pre_c(x, g1, mod, w_in_pad, q_g, kv_g, w_uq_pad, cos_t, sin_t):
    def tab(i):
        return (jnp.where(i < NP // TM, 0, 1 + ((i * TM - NP) % DEC_SEQ) // TM), 0)
    return pl.pallas_call(
        _pre_c_kernel,
        grid=(N_TOK // TM,),
        in_specs=[pl.BlockSpec((TM, D_MODEL), lambda i: (i, 0)),
                  pl.BlockSpec((1, D_MODEL), lambda i: (0, 0)),
                  pl.BlockSpec((1, 6, D_MODEL), lambda i: (_seg(i, TM), 0, 0)),
                  pl.BlockSpec((D_MODEL, 512), lambda i: (0, 0)),
                  pl.BlockSpec((1, MLA_Q_RANK), lambda i: (0, 0)),
                  pl.BlockSpec((1, MLA_KV_RANK), lambda i: (0, 0)),
                  pl.BlockSpec((MLA_Q_RANK, MLA_HEADS * LANES), lambda i: (0, 0)),
                  pl.BlockSpec((TM, LANES), tab),
                  pl.BlockSpec((TM, LANES), tab)],
        out_specs=[pl.BlockSpec((TM, MLA_HEADS * LANES), lambda i: (i, 0)),
                   pl.BlockSpec((TM, MLA_KV_RANK), lambda i: (i, 0)),
                   pl.BlockSpec((TM, LANES), lambda i: (i, 0)),
                   pl.BlockSpec((TM, LANES), lambda i: (i, 0))],
        out_shape=[jax.ShapeDtypeStruct((N_TOK, MLA_HEADS * LANES), BF16),
                   jax.ShapeDtypeStruct((N_TOK, MLA_KV_RANK), F32),
                   jax.ShapeDtypeStruct((N_TOK, LANES), F32),
                   jax.ShapeDtypeStruct((N_TOK, LANES), F32)],
        compiler_params=_params(("arbitrary",)),
        name="pre_c",
    )(x, g1.reshape(1, D_MODEL), mod, w_in_pad, q_g.reshape(1, -1), kv_g.reshape(1, -1), w_uq_pad,
      cos_t, sin_t)


def _s5_tables(a_re, a_im, log_dt, b_re, b_im, c_re, c_im):
    C = SSM_CHUNK
    a_re = jnp.minimum(a_re.astype(F32), -1e-4)
    a_im = a_im.astype(F32)
    dt = jnp.exp(log_dt.astype(F32))[..., None]
    mag = jnp.exp(dt * a_re)
    ab_re = mag * jnp.cos(dt * a_im)
    ab_im = mag * jnp.sin(dt * a_im)
    den = a_re * a_re + a_im * a_im
    f_re = ((ab_re - 1.0) * a_re + ab_im * a_im) / den
    f_im = (ab_im * a_re - (ab_re - 1.0) * a_im) / den
    b_re = b_re.astype(F32)
    b_im = b_im.astype(F32)
    bb_re = f_re[..., None] * b_re - f_im[..., None] * b_im
    bb_im = f_re[..., None] * b_im + f_im[..., None] * b_re
    ks = jnp.arange(C + 1, dtype=F32)[:, None, None, None]
    pmag = jnp.exp(ks * (dt * a_re)[None])
    p_re = pmag * jnp.cos(ks * (dt * a_im)[None])
    p_im = pmag * jnp.sin(ks * (dt * a_im)[None])
    c_re = c_re.astype(F32)[None]
    c_im = c_im.astype(F32)[None]
    cl_re = c_re * p_re[:, :, :, None, :] - c_im * p_im[:, :, :, None, :]
    cl_im = c_re * p_im[:, :, :, None, :] + c_im * p_re[:, :, :, None, :]
    kmat = (jnp.einsum('kdgpn,dgnq->kdgpq', cl_re[:C], bb_re, precision=HIGHEST)
            - jnp.einsum('kdgpn,dgnq->kdgpq', cl_im[:C], bb_im, precision=HIGHEST))
    lag = np.arange(C)[None, :] - np.arange(C)[:, None]
    sel_f = (lag[:, :, None] == np.arange(C)).reshape(C * C, C).astype(np.float32)
    sel_b = (-lag[:, :, None] == np.arange(C)).reshape(C * C, C).astype(np.float32)
    t = (jnp.dot(sel_f, kmat[:, 0].reshape(C, -1), precision=HIGHEST)
         + jnp.dot(sel_b, kmat[:, 1].reshape(C, -1), precision=HIGHEST))
    m_loc = (t.reshape(C, C, SSM_GROUPS, SSM_GROUP, SSM_GROUP).transpose(2, 0, 4, 1, 3)
             .reshape(SSM_GROUPS, C * SSM_GROUP, C * SSM_GROUP))

    def state_in(d, flip):
        pr = (p_re[:C, d][::-1] if flip else p_re[:C, d])[..., None]
        pi = (p_im[:C, d][::-1] if flip else p_im[:C, d])[..., None]
        er = pr * bb_re[d][None] - pi * bb_im[d][None]
        ei = pr * bb_im[d][None] + pi * bb_re[d][None]
        to = lambda e: e.transpose(1, 0, 3, 2).reshape(SSM_GROUPS, C * SSM_GROUP, SSM_STATE)
        return to(er), to(ei)

    ef_re, ef_im = state_in(0, True)
    eb_re, eb_im = state_in(1, False)
    w1 = jnp.concatenate([m_loc, ef_re, eb_re, ef_im, eb_im], axis=-1)

    def state_out(d, flip):
        to = lambda e: e.transpose(1, 3, 0, 2).reshape(SSM_GROUPS, SSM_STATE, C * SSM_GROUP)
        cr = cl_re[1:C + 1, d][::-1] if flip else cl_re[1:C + 1, d]
        ci = cl_im[1:C + 1, d][::-1] if flip else cl_im[1:C + 1, d]
        return to(cr), to(-ci)

    ff_re, ff_im = state_out(0, False)
    fb_re, fb_im = state_out(1, True)
    w2 = jnp.concatenate([ff_re, fb_re, ff_im, fb_im], axis=1)
    a16 = jnp.stack([jnp.concatenate([p_re[C, 0], p_re[C, 1]], axis=-1),
                     jnp.concatenate([p_im[C, 0], p_im[C, 1]], axis=-1)], axis=1)
    return w1.astype(BF16), w2.astype(BF16), a16


S5_GB = LANES // SSM_GROUP


def _s5_kernel(nb, nc, u_ref, w1_ref, w2_ref, a_ref, x0_ref, y_ref, xf_ref,
               ug_ref, yl_ref, lre_ref, lim_ref, s_ref, l2re_ref, l2im_ref):
    r = nb * nc
    for s in range(SSM_CHUNK):
        piece = u_ref[pl.ds(s, r, stride=SSM_CHUNK), :]
        for gl in range(S5_GB):
            ug_ref[gl, :, s * SSM_GROUP:(s + 1) * SSM_GROUP] = piece[:, gl * SSM_GROUP:(gl + 1) * SSM_GROUP]
    fwd = lax.broadcasted_iota(jnp.int32, (nb, LANES), 1) < SSM_STATE

    def group(gl, carry):
        r1 = _bdot(ug_ref[gl], w1_ref[gl])
        yl_ref[gl] = r1[:, 0:256]
        lre_ref[...] = r1[:, 256:384]
        lim_ref[...] = r1[:, 384:512]
        a = a_ref[gl]
        a_re = a[0:1]
        a_im = a[1:2]
        s_re = x0_ref[gl, 0]
        s_im = x0_ref[gl, 1]
        s_ref[0, 0] = s_re
        s_ref[1, 0] = s_im
        for k in range(nc):
            rf = pl.ds(k, nb, stride=nc)
            rb = pl.ds(nc - 1 - k, nb, stride=nc)
            l_re = jnp.where(fwd, lre_ref[rf, :], lre_ref[rb, :])
            l_im = jnp.where(fwd, lim_ref[rf, :], lim_ref[rb, :])
            n_re = a_re * s_re - a_im * s_im + l_re
            n_im = a_re * s_im + a_im * s_re + l_im
            s_re, s_im = n_re, n_im
            s_ref[0, k + 1] = s_re
            s_ref[1, k + 1] = s_im
        xf_ref[gl, 0] = s_re
        xf_ref[gl, 1] = s_im
        for c in range(nc):
            rows = pl.ds(c, nb, stride=nc)
            l2re_ref[rows, :] = jnp.where(fwd, s_ref[0, c], s_ref[0, nc - 1 - c])
            l2im_ref[rows, :] = jnp.where(fwd, s_ref[1, c], s_ref[1, nc - 1 - c])
        l2 = jnp.concatenate([l2re_ref[...], l2im_ref[...]], axis=-1)
        yl_ref[gl] = yl_ref[gl] + _bdot(l2, w2_ref[gl])
        return carry

    lax.fori_loop(0, S5_GB, group, 0)
    for j in range(SSM_CHUNK):
        tile = jnp.concatenate([yl_ref[gl, :, j * SSM_GROUP:(j + 1) * SSM_GROUP] for gl in range(S5_GB)],
                               axis=-1)
        y_ref[pl.ds(j, r, stride=SSM_CHUNK), :] = tile


def _s5(u4, w1, w2, a16, x0, nb, nc, grp):
    r = nb * nc
    t = r * SSM_CHUNK
    nblk = SSM_WIDTH // LANES
    return pl.pallas_call(
        functools.partial(_s5_kernel, nb, nc),
        grid=(nblk,),
        in_specs=[pl.BlockSpec((None, t, LANES), lambda cb: (cb, grp, 0)),
                  pl.BlockSpec((S5_GB, 256, 512), lambda cb: (cb, 0, 0)),
                  pl.BlockSpec((S5_GB, 256, 256), lambda cb: (cb, 0, 0)),
                  pl.BlockSpec((S5_GB, 2, LANES), lambda cb: (cb, 0, 0)),
                  pl.BlockSpec((S5_GB, 2, nb, LANES), lambda cb: (cb, 0, 0, 0))],
        out_specs=[pl.BlockSpec((None, t, LANES), lambda cb: (cb, 0, 0)),
                   pl.BlockSpec((S5_GB, 2, nb, LANES), lambda cb: (cb, 0, 0, 0))],
        out_shape=[jax.ShapeDtypeStruct((nblk, t, LANES), F32),
                   jax.ShapeDtypeStruct((SSM_GROUPS, 2, nb, LANES), F32)],
        scratch_shapes=[pltpu.VMEM((S5_GB, r, 256), F32),
                        pltpu.VMEM((S5_GB, r, 256), F32),
                        pltpu.VMEM((r, LANES), F32),
                        pltpu.VMEM((r, LANES), F32),
                        pltpu.VMEM((2, nc + 1, nb, LANES), F32),
                        pltpu.VMEM((r, LANES), F32),
                        pltpu.VMEM((r, LANES), F32)],
        compiler_params=_params(("arbitrary",)),
        name="s5_chunked",
    )(u4, w1, w2, a16, x0)


def _ret_tables(logit):
    C = RET_CHUNK
    lg = jax.nn.log_sigmoid(logit.astype(F32))
    pos = jnp.arange(C, dtype=F32)
    diff = pos[:, None] - pos[None, :]
    dsum = (jnp.where(diff >= 0, jnp.exp(jnp.maximum(diff, 0.0)[None] * lg[0][:, None, None]), 0.0)
            + jnp.where(diff <= 0, jnp.exp(jnp.maximum(-diff, 0.0)[None] * lg[1][:, None, None]), 0.0))
    kdf = jnp.exp((C - 1 - pos)[:, None] * lg[0][None, :])
    kdb = jnp.exp(pos[:, None] * lg[1][None, :])
    qdf = jnp.exp((pos + 1)[:, None] * lg[0][None, :])
    qdb = jnp.exp((C - pos)[:, None] * lg[1][None, :])
    dec = jnp.stack([jnp.repeat(t, RET_QK_DIM, axis=1) for t in (kdf, kdb, qdf, qdb)])
    cd = jnp.exp(C * lg)
    return dsum, dec, cd


def _ab_core_kernel(seq, use_pos, with_state, *refs):
    it = iter(refs)
    qkvg_ref, u_ref, ycv_ref, dsum_ref, dec_ref, cd_ref, d_ref, wglu_ref = [next(it) for _ in range(8)]
    if use_pos:
        cos_ref, sin_ref = next(it), next(it)
    s0_ref = next(it) if use_pos else None
    next(it)
    mix_ref = next(it)
    st_ref = next(it) if with_state else None
    C = RET_CHUNK
    n = seq // C
    q = qkvg_ref[:, 0:256] * (RET_QK_DIM ** -0.5)
    k = qkvg_ref[:, 256:512]
    if use_pos:
        cos = cos_ref[...]
        sins = sin_ref[...]
        first = (lax.broadcasted_iota(jnp.int32, cos.shape, 1) % 32) < 16
        q = _rope_lanes(q, cos, sins, first, 16)
        k = _rope_lanes(k, cos, sins, first, 16)
    kdf, kdb, qdf, qdb = dec_ref[0], dec_ref[1], dec_ref[2], dec_ref[3]
    for h in range(RET_HEADS):
        qs = slice(h * RET_QK_DIM, (h + 1) * RET_QK_DIM)
        vs = slice(512 + h * RET_V_DIM, 512 + (h + 1) * RET_V_DIM)
        gs = slice(1024 + h * RET_V_DIM, 1024 + (h + 1) * RET_V_DIM)
        qc, kc, vc, kvf, kvb = [], [], [], [], []
        for c in range(n):
            rows = slice(c * C, (c + 1) * C)
            qc.append(q[rows, qs])
            kc.append(k[rows, qs])
            vc.append(qkvg_ref[rows, vs])
            kvf.append(_bdot((kc[c] * kdf[:, qs]).T, vc[c]))
            kvb.append(_bdot((kc[c] * kdb[:, qs]).T, vc[c]))
        if use_pos:
            sf = s0_ref[0, 0, h]
            sb = s0_ref[0, 1, h]
        else:
            sf = jnp.zeros((RET_QK_DIM, RET_V_DIM), F32)
            sb = jnp.zeros((RET_QK_DIM, RET_V_DIM), F32)
        prev_f = []
        for c in range(n):
            prev_f.append(sf)
            sf = cd_ref[0, h] * sf + kvf[c]
        next_b = [None] * n
        for c in range(n - 1, -1, -1):
            next_b[c] = sb
            sb = cd_ref[1, h] * sb + kvb[c]
        if with_state:
            st_ref[0, 0, h] = sf
            st_ref[0, 1, h] = sb
        for c in range(n):
            rows = slice(c * C, (c + 1) * C)
            sc = lax.dot_general(qc[c].astype(BF16), kc[c].astype(BF16), (((1,), (1,)), ((), ())),
                                 preferred_element_type=F32) * dsum_ref[h]
            o = (_bdot(sc, vc[c]) + _bdot(qc[c] * qdf[:, qs], prev_f[c])
                 + _bdot(qc[c] * qdb[:, qs], next_b[c]))
            o = o * lax.rsqrt(jnp.mean(o * o, axis=-1, keepdims=True) + NORM_EPS)
            g = qkvg_ref[rows, gs]
            mix_ref[rows, h * RET_V_DIM:(h + 1) * RET_V_DIM] = (o * (g * jax.nn.sigmoid(g))).astype(BF16)
    nblk = SSM_WIDTH // LANES
    u = jnp.concatenate([u_ref[kb] for kb in range(nblk)], axis=-1)
    ycv = jnp.concatenate([ycv_ref[kb] for kb in range(nblk)], axis=-1)
    y = ycv + d_ref[...] * u
    z = 0.5 * y * (1.0 + jnp.tanh(math.sqrt(2.0 / math.pi) * (y + 0.044715 * (y * y * y))))
    z = z * jax.nn.sigmoid(_bdot(z, wglu_ref[...]))
    mix_ref[:, 512:1024] = z.astype(BF16)


def _ab_core(qkvg, u, ycv, dsum, dec, cd, d, w_glu, mix_prev, *, nb, seq, row0, rope=None, s0=None):
    use_pos = rope is not None
    with_state = not use_pos
    blk0 = row0 // seq
    in_specs = [pl.BlockSpec((seq, 1536), lambda b: (blk0 + b, 0)),
                pl.BlockSpec((SSM_WIDTH // LANES, seq, LANES), lambda b: (0, blk0 + b, 0)),
                pl.BlockSpec((SSM_WIDTH // LANES, seq, LANES), lambda b: (0, b, 0)),
                pl.BlockSpec((RET_HEADS, RET_CHUNK, RET_CHUNK), lambda b: (0, 0, 0)),
                pl.BlockSpec((4, RET_CHUNK, 256), lambda b: (0, 0, 0)),
                pl.BlockSpec(memory_space=pltpu.SMEM),
                pl.BlockSpec((1, SSM_WIDTH), lambda b: (0, 0)),
                pl.BlockSpec((SSM_WIDTH, SSM_WIDTH), lambda b: (0, 0))]
    args = [qkvg, u, ycv, dsum, dec, cd, d.reshape(1, SSM_WIDTH), w_glu]
    if use_pos:
        in_specs += [pl.BlockSpec((seq, 256), lambda b: (0, 0)),
                     pl.BlockSpec((seq, 256), lambda b: (0, 0)),
                     pl.BlockSpec((1, 2, RET_HEADS, RET_QK_DIM, RET_V_DIM), lambda b: (b, 0, 0, 0, 0))]
        args += [rope[0], rope[1], s0]
    in_specs.append(pl.BlockSpec(memory_space=pl.ANY))
    args.append(mix_prev)
    out_specs = [pl.BlockSpec((seq, D_MODEL), lambda b: (blk0 + b, 0))]
    out_shape = [jax.ShapeDtypeStruct((N_TOK, D_MODEL), BF16)]
    if with_state:
        out_specs.append(pl.BlockSpec((1, 2, RET_HEADS, RET_QK_DIM, RET_V_DIM), lambda b: (b, 0, 0, 0, 0)))
        out_shape.append(jax.ShapeDtypeStruct((nb, 2, RET_HEADS, RET_QK_DIM, RET_V_DIM), F32))
    return pl.pallas_call(
        functools.partial(_ab_core_kernel, seq, use_pos, with_state),
        grid=(nb,),
        in_specs=in_specs,
        out_specs=out_specs,
        out_shape=out_shape,
        input_output_aliases={len(args) - 1: 0},
        compiler_params=_params(("arbitrary",)),
        name="ab_core_pos" if use_pos else "ab_core",
    )(*args)


def _attn_kernel(lk, tq, q_ref, ckv_ref, kr_ref, wukv_ref, _, o_ref, k_sc, v_sc):
    @pl.when(pl.program_id(1) == 0)
    def _():
        kv = _bdot(ckv_ref[0], wukv_ref[...])
        kr = kr_ref[0]
        scale = (MLA_NOPE + MLA_ROPE) ** -0.5
        for h in range(MLA_HEADS):
            sl = slice(h * LANES, (h + 1) * LANES)
            k_sc[:, sl] = ((kv[:, sl] + kr) * scale).astype(BF16)
        v_sc[...] = kv[:, MLA_HEADS * LANES:].astype(BF16)

    for h in range(MLA_HEADS):
        sl = slice(h * LANES, (h + 1) * LANES)
        s = lax.dot_general(q_ref[:, sl], k_sc[:, sl], (((1,), (1,)), ((), ())),
                            preferred_element_type=F32)
        e = jnp.exp(s - jnp.max(s, axis=-1, keepdims=True))
        o = _bdot(e, v_sc[:, sl]) / jnp.sum(e, axis=-1, keepdims=True)
        o_ref[:, sl] = o.astype(BF16)


def _attention(q, ckv_all, kr_all, w_ukv_pad, o_prev, *, nb, seq, lk, row0, tq=256):
    nq = seq // tq
    blk0 = row0 // tq
    return pl.pallas_call(
        functools.partial(_attn_kernel, lk, tq),
        grid=(nb, nq),
        in_specs=[pl.BlockSpec((tq, MLA_HEADS * LANES), lambda b, i: (blk0 + b * nq + i, 0)),
                  pl.BlockSpec((1, lk, MLA_KV_RANK), lambda b, i: (b, 0, 0)),
                  pl.BlockSpec((1, lk, LANES), lambda b, i: (b, 0, 0)),
                  pl.BlockSpec((MLA_KV_RANK, 2 * MLA_HEADS * LANES), lambda b, i: (0, 0)),
                  pl.BlockSpec(memory_space=pl.ANY)],
        out_specs=pl.BlockSpec((tq, MLA_HEADS * MLA_V), lambda b, i: (blk0 + b * nq + i, 0)),
        out_shape=jax.ShapeDtypeStruct((N_TOK, MLA_HEADS * MLA_V), BF16),
        scratch_shapes=[pltpu.VMEM((lk, MLA_HEADS * LANES), BF16),
                        pltpu.VMEM((lk, MLA_HEADS * MLA_V), BF16)],
        input_output_aliases={4: 0},
        compiler_params=_params(("arbitrary", "arbitrary")),
        name="mla_attn_%d" % lk,
    )(q, ckv_all, kr_all, w_ukv_pad, o_prev)


def _post_kernel(mix_ref, x_ref, mod_ref, g_ref, wo_ref, rw_ref, rb_ref,
                 xmid_ref, h2_ref, route_ref, cnt_ref, base_ref):
    i = pl.program_id(0)

    @pl.when(i == 0)
    def _():
        base_ref[...] = jnp.zeros_like(base_ref)

    m = mod_ref[0]
    xm = x_ref[...] + m[2:3] * jnp.dot(mix_ref[...], wo_ref[...], preferred_element_type=F32)
    xmid_ref[...] = xm
    h2 = _rms(xm, g_ref[...]) * (1.0 + m[4:5]) + m[3:4]
    _store_slabs(h2_ref, 0, h2)
    logits = _dot3(h2, rw_ref[...]) + rb_ref[...]
    lane_f = lax.broadcasted_iota(jnp.int32, logits.shape, 1).astype(F32)
    l = logits
    vals, sels = [], []
    for _k in range(TOP_K):
        mx = jnp.max(l, axis=-1, keepdims=True)
        idx = jnp.min(jnp.where(l == mx, lane_f, float(N_EXPERTS)), axis=-1, keepdims=True)
        sel = lane_f == idx
        vals.append(mx)
        sels.append(sel)
        l = jnp.where(sel, -jnp.inf, l)
    ex = [jnp.exp(v - vals[0]) for v in vals]
    den = ex[0] + ex[1] + ex[2] + ex[3]
    onehot = (sels[0] | sels[1] | sels[2] | sels[3]).astype(F32)
    r_i = lax.broadcasted_iota(jnp.int32, (TM, TM), 0)
    c_i = lax.broadcasted_iota(jnp.int32, (TM, TM), 1)
    tri = (c_i < r_i).astype(BF16)
    before = base_ref[...] + jnp.dot(tri, onehot.astype(BF16), preferred_element_type=F32)
    out_lane = lax.broadcasted_iota(jnp.int32, (TM, LANES), 1)
    route = jnp.zeros((TM, LANES), F32)
    for kk in range(TOP_K):
        sel_f = sels[kk].astype(F32)
        eid = jnp.sum(sel_f * lane_f, axis=-1, keepdims=True)
        pos = jnp.sum(sel_f * before, axis=-1, keepdims=True)
        route = jnp.where(out_lane == kk, eid, route)
        route = jnp.where(out_lane == TOP_K + kk, ex[kk] / den, route)
        route = jnp.where(out_lane == 2 * TOP_K + kk, pos, route)
    route_ref[...] = route
    base_ref[...] = base_ref[...] + jnp.sum(onehot, axis=0, keepdims=True)
    cnt_ref[...] = base_ref[...]


def _post(mix, x, mod, g2, w_out, router_w, router_b):
    return pl.pallas_call(
        _post_kernel,
        grid=(N_TOK // TM,),
        in_specs=[pl.BlockSpec((TM, D_MODEL), lambda i: (i, 0)),
                  pl.BlockSpec((TM, D_MODEL), lambda i: (i, 0)),
                  pl.BlockSpec((1, 6, D_MODEL), lambda i: (_seg(i, TM), 0, 0)),
                  pl.BlockSpec((1, D_MODEL), lambda i: (0, 0)),
                  pl.BlockSpec((D_MODEL, D_MODEL), lambda i: (0, 0)),
                  pl.BlockSpec((D_MODEL, N_EXPERTS), lambda i: (0, 0)),
                  pl.BlockSpec((1, N_EXPERTS), lambda i: (0, 0))],
        out_specs=[pl.BlockSpec((TM, D_MODEL), lambda i: (i, 0)),
                   pl.BlockSpec((TM * SLAB, LANES), lambda i: (i, 0)),
                   pl.BlockSpec((TM, LANES), lambda i: (i, 0)),
                   pl.BlockSpec((1, N_EXPERTS), lambda i: (0, 0))],
        out_shape=[jax.ShapeDtypeStruct((N_TOK, D_MODEL), F32),
                   jax.ShapeDtypeStruct((N_TOK * SLAB, LANES), F32),
                   jax.ShapeDtypeStruct((N_TOK, LANES), F32),
                   jax.ShapeDtypeStruct((1, N_EXPERTS), F32)],
        scratch_shapes=[pltpu.VMEM((1, N_EXPERTS), F32)],
        compiler_params=_params(("arbitrary",)),
        name="post_router",
    )(mix, x, mod, g2.reshape(1, D_MODEL), w_out, router_w, router_b.reshape(1, N_EXPERTS))


def _route_plan(route, counts):
    eid = route[:, 0:TOP_K].astype(jnp.int32)
    pos = route[:, 2 * TOP_K:3 * TOP_K].astype(jnp.int32)
    cnt = counts[0].astype(jnp.int32)
    blocks = (cnt + TME - 1) // TME
    blk_end = jnp.cumsum(blocks)
    pad_start = (blk_end - blocks) * TME
    experts = jnp.arange(N_EXPERTS, dtype=jnp.int32)
    dest = jnp.sum(jnp.where(eid[:, :, None] == experts, pad_start, 0), axis=-1) + pos
    n_used = blk_end[-1]
    blk = jnp.arange(N_EBLK, dtype=jnp.int32)
    blk_e = jnp.sum((jnp.minimum(blk, n_used - 1)[:, None] >= blk_end[None, :]).astype(jnp.int32), axis=-1)
    blk_e = jnp.minimum(blk_e, N_EXPERTS - 1).astype(jnp.int32)
    is_last = jnp.any((blk[:, None] == blk_end[None, :] - 1) & (blocks[None, :] > 0), axis=-1)
    zero_blk = (is_last | (blk >= n_used)).astype(jnp.int32)
    return dest.astype(jnp.int32), blk_e, n_used.reshape(1).astype(jnp.int32), zero_blk


def _dispatch_kernel(zero_ref, dest_ref, h2_ref, xs_hbm, zbuf, sem, zsem):
    i = pl.program_id(0)

    @pl.when(i == 0)
    def _():
        zbuf[...] = jnp.zeros_like(zbuf)

        def zero_fill(wait):
            def body(j, carry):
                @pl.when(zero_ref[j] != 0)
                def _():
                    cp = pltpu.make_async_copy(
                        zbuf, xs_hbm.at[pl.ds(pl.multiple_of(j * (TME * SLAB), TME * SLAB), TME * SLAB)], zsem)
                    if wait:
                        cp.wait()
                    else:
                        cp.start()
                return carry
            lax.fori_loop(0, N_EBLK, body, 0)

        zero_fill(False)
        zero_fill(True)

    def scatter(wait):
        def body(r, carry):
            for kk in range(TOP_K):
                cp = _slab_copy(h2_ref, xs_hbm, sem, r, dest_ref[0, 0, r * TOP_K + kk])
                if wait:
                    cp.wait()
                else:
                    cp.start(priority=kk % 2)
            return carry
        lax.fori_loop(0, TMC, body, 0, unroll=4)

    scatter(False)
    scatter(True)


def _dispatch(h2, dest3, zero_blk):
    nt = N_TOK // TMC
    grid_spec = pltpu.PrefetchScalarGridSpec(
        num_scalar_prefetch=1,
        grid=(nt,),
        in_specs=[pl.BlockSpec((1, 1, TMC * TOP_K), lambda i, z: (i, 0, 0), memory_space=pltpu.SMEM),
                  pl.BlockSpec((TMC * SLAB, LANES), lambda i, z: (i, 0))],
        out_specs=pl.BlockSpec(memory_space=pl.ANY),
        scratch_shapes=[pltpu.VMEM((TME * SLAB, LANES), F32),
                        pltpu.SemaphoreType.DMA(()),
                        pltpu.SemaphoreType.DMA(())])
    return pl.pallas_call(
        _dispatch_kernel,
        grid_spec=grid_spec,
        out_shape=jax.ShapeDtypeStruct((P_ROWS * SLAB, LANES), F32),
        compiler_params=_params(("arbitrary",)),
        name="moe_dispatch",
    )(zero_blk, dest3, h2)


def _moe_kernel(blk_e_ref, nused_ref, x_ref, win_ref, bin_ref, wout_ref, bout_ref, y_ref, win_bf, wout_bf):
    j = pl.program_id(0)
    n_used = nused_ref[0]

    @pl.when(j < n_used)
    def _():
        changed = jnp.logical_or(j == 0, blk_e_ref[j] != blk_e_ref[jnp.maximum(j - 1, 0)])

        @pl.when(changed)
        def _():
            win_bf[...] = win_ref[0].astype(BF16)
            wout_bf[...] = wout_ref[0].astype(BF16)

        xb = _load_slabs(x_ref, 0, TME).astype(BF16)
        hb = jnp.dot(xb, win_bf[...], preferred_element_type=F32) + bin_ref[0]
        glu = jnp.minimum(hb[:, :D_EXPERT], SWIGLU_LIMIT)
        lin = jnp.clip(hb[:, D_EXPERT:], -SWIGLU_LIMIT, SWIGLU_LIMIT)
        act = glu * jax.nn.sigmoid(SWIGLU_ALPHA * glu) * (lin + 1.0)
        y = jnp.dot(act.astype(BF16), wout_bf[...], preferred_element_type=F32) + bout_ref[0]
        _store_slabs(y_ref, 0, y)

    @pl.when(j >= n_used)
    def _():
        y_ref[...] = jnp.zeros_like(y_ref)


def _moe(xs, blk_e, n_used, w_in, b_in, w_out, b_out, layer):
    e0 = layer * N_EXPERTS
    grid_spec = pltpu.PrefetchScalarGridSpec(
        num_scalar_prefetch=2,
        grid=(N_EBLK,),
        in_specs=[pl.BlockSpec((TME * SLAB, LANES), lambda j, be, nu: (jnp.minimum(j, nu[0] - 1), 0)),
                  pl.BlockSpec((1, D_MODEL, 2 * D_EXPERT), lambda j, be, nu: (e0 + be[j], 0, 0)),
                  pl.BlockSpec((1, 1, 2 * D_EXPERT), lambda j, be, nu: (e0 + be[j], 0, 0)),
                  pl.BlockSpec((1, D_EXPERT, D_MODEL), lambda j, be, nu: (e0 + be[j], 0, 0)),
                  pl.BlockSpec((1, 1, D_MODEL), lambda j, be, nu: (e0 + be[j], 0, 0))],
        out_specs=pl.BlockSpec((TME * SLAB, LANES), lambda j, be, nu: (j, 0)),
        scratch_shapes=[pltpu.VMEM((D_MODEL, 2 * D_EXPERT), BF16),
                        pltpu.VMEM((D_EXPERT, D_MODEL), BF16)])
    return pl.pallas_call(
        _moe_kernel,
        grid_spec=grid_spec,
        out_shape=jax.ShapeDtypeStruct((P_ROWS * SLAB, LANES), F32),
        compiler_params=_params(("arbitrary",)),
        name="moe_ffn",
    )(blk_e, n_used, xs, w_in, b_in, w_out, b_out)


def _combine_tile(tm, dest_ref, destn_ref, y_hbm, xmid_ref, mod_ref, route_ref, buf, sem):
    i = pl.program_id(0)
    n = pl.num_programs(0)
    slot = i % 2

    def gather(idx_ref, s, wait):
        def body(r, carry):
            for kk in range(TOP_K):
                cp = _slab_copy(y_hbm, buf, sem.at[s], idx_ref[0, 0, r * TOP_K + kk],
                                (s * TOP_K + kk) * tm + r)
                if wait:
                    cp.wait()
                else:
                    cp.start(priority=kk % 2)
            return carry
        lax.fori_loop(0, tm, body, 0, unroll=4)

    @pl.when(i == 0)
    def _():
        gather(dest_ref, 0, False)

    @pl.when(i + 1 < n)
    def _():
        gather(destn_ref, 1 - slot, False)

    gather(dest_ref, slot, True)
    gates = route_ref[...]
    acc = gates[:, TOP_K:TOP_K + 1] * _load_slabs(buf, slot * TOP_K * tm, tm)
    for kk in range(1, TOP_K):
        acc = acc + gates[:, TOP_K + kk:TOP_K + kk + 1] * _load_slabs(buf, (slot * TOP_K + kk) * tm, tm)
    return xmid_ref[...] + mod_ref[0][5:6] * acc


def _combine_specs(tm):
    nt = N_TOK // tm
    in_specs = [pl.BlockSpec((1, 1, tm * TOP_K), lambda i: (i, 0, 0), memory_space=pltpu.SMEM),
                pl.BlockSpec((1, 1, tm * TOP_K), lambda i: (jnp.minimum(i + 1, nt - 1), 0, 0),
                             memory_space=pltpu.SMEM),
                pl.BlockSpec(memory_space=pl.ANY),
                pl.BlockSpec((tm, D_MODEL), lambda i: (i, 0)),
                pl.BlockSpec((1, 6, D_MODEL), lambda i: (_seg(i, tm), 0, 0)),
                pl.BlockSpec((tm, LANES), lambda i: (i, 0))]
    scratch = [pltpu.VMEM((2 * TOP_K * tm * SLAB, LANES), F32), pltpu.SemaphoreType.DMA((2,))]
    return in_specs, scratch


def _combine_kernel(final, dest_ref, destn_ref, y_hbm, xmid_ref, mod_ref, route_ref, fg_ref, o_ref, buf, sem):
    out = _combine_tile(TMC, dest_ref, destn_ref, y_hbm, xmid_ref, mod_ref, route_ref, buf, sem)
    o_ref[...] = _rms(out, fg_ref[...]) if final else out


def _combine(y, dest3, xmid, mod, route, final_g, final):
    in_specs, scratch = _combine_specs(TMC)
    return pl.pallas_call(
        functools.partial(_combine_kernel, final),
        grid=(N_TOK // TMC,),
        in_specs=in_specs + [pl.BlockSpec((1, D_MODEL), lambda i: (0, 0))],
        out_specs=pl.BlockSpec((TMC, D_MODEL), lambda i: (i, 0)),
        out_shape=jax.ShapeDtypeStruct((N_TOK, D_MODEL), F32),
        scratch_shapes=scratch,
        compiler_params=_params(("arbitrary",)),
        name="moe_combine_final" if final else "moe_combine",
    )(dest3, dest3, y, xmid, mod, route, final_g.reshape(1, D_MODEL))


def _axial_tables(seq, dim, lanes, offset):
    half = dim // 2
    nf = half // 2
    t = jnp.arange(seq)
    row = (t // GRID_W).astype(F32)
    col = (t % GRID_W).astype(F32)
    inv = ROPE_BASE ** (-jnp.arange(nf, dtype=F32) / nf)
    d = jnp.arange(lanes) - offset
    inside = (d >= 0) & (d < dim)
    dd = jnp.clip(d, 0, dim - 1)
    w = dd % half
    pos = jnp.where((dd // half)[None, :] == 0, row[:, None], col[:, None])
    ang = pos * inv[w % nf][None, :]
    sign = jnp.where(w < nf, -1.0, 1.0)[None, :]
    cos = jnp.where(inside[None, :], jnp.cos(ang), 1.0)
    sins = jnp.where(inside[None, :], sign * jnp.sin(ang), 0.0)
    return cos.astype(F32), sins.astype(F32)


def kernel(x_prompt, x_sample, c, state_ret, state_ssm_re, state_ssm_im, cache_mla_ckv, cache_mla_krope, c_ctx, w_ada, b_ada, norm1_g, norm2_g, final_norm_g, w_in_ab, w_out_ab, ret_decay_logit, ssm_a_re, ssm_a_im, ssm_log_dt, ssm_b_re, ssm_b_im, ssm_c_re, ssm_c_im, ssm_d, ssm_w_glu, w_in_c, mla_q_norm_g, mla_kv_norm_g, mla_w_uq, mla_w_ukv, w_out_c, router_w, router_b, moe_w_in, moe_b_in, moe_w_out, moe_b_out):
    x = jnp.concatenate([x_prompt.reshape(NP, D_MODEL), x_sample.reshape(NS, D_MODEL)], axis=0)
    mods = _modulation(c_ctx, c, w_ada, b_ada)

    rc, rs = _axial_tables(DEC_SEQ, RET_QK_DIM, RET_QK_DIM, 0)
    ret_rope = (jnp.tile(rc, (1, RET_HEADS)), jnp.tile(rs, (1, RET_HEADS)))
    mc, ms = _axial_tables(DEC_SEQ, MLA_ROPE, LANES, MLA_NOPE)
    mla_cos = jnp.concatenate([jnp.ones((TM, LANES), F32), mc], axis=0)
    mla_sin = jnp.concatenate([jnp.zeros((TM, LANES), F32), ms], axis=0)

    w_in_all = moe_w_in.reshape(DEPTH * N_EXPERTS, D_MODEL, 2 * D_EXPERT)
    b_in_all = moe_b_in.reshape(DEPTH * N_EXPERTS, 1, 2 * D_EXPERT)
    w_out_all = moe_w_out.reshape(DEPTH * N_EXPERTS, D_EXPERT, D_MODEL)
    b_out_all = moe_b_out.reshape(DEPTH * N_EXPERTS, 1, D_MODEL)

    s5_w1, s5_w2, s5_a16 = jax.vmap(_s5_tables)(ssm_a_re, ssm_a_im, ssm_log_dt, ssm_b_re, ssm_b_im,
                                                ssm_c_re, ssm_c_im)
    ret_dsum, ret_dec, ret_cd = jax.vmap(_ret_tables)(ret_decay_logit)
    w_in_ab_bf = w_in_ab.astype(BF16)
    w_glu_bf = ssm_w_glu.astype(BF16)
    w_out_bf = jnp.stack([w_out_ab, w_out_c], axis=1).reshape(DEPTH, D_MODEL, D_MODEL).astype(BF16)
    n_c = w_in_c.shape[0]
    cut = MLA_Q_RANK + MLA_KV_RANK
    w_in_c_pad = jnp.concatenate(
        [w_in_c[:, :, :cut], jnp.zeros((n_c, D_MODEL, MLA_NOPE), F32),
         w_in_c[:, :, cut:], jnp.zeros((n_c, D_MODEL, LANES - MLA_NOPE - MLA_ROPE), F32)], axis=2).astype(BF16)
    wq = mla_w_uq.reshape(n_c, MLA_Q_RANK, MLA_HEADS, MLA_NOPE + MLA_ROPE)
    w_uq_pad_all = jnp.pad(wq, ((0, 0), (0, 0), (0, 0), (0, LANES - MLA_NOPE - MLA_ROPE))).reshape(
        n_c, MLA_Q_RANK, MLA_HEADS * LANES).astype(BF16)
    wkv = mla_w_ukv.reshape(n_c, MLA_KV_RANK, MLA_HEADS, MLA_NOPE + MLA_V)
    wk = jnp.pad(wkv[..., :MLA_NOPE], ((0, 0), (0, 0), (0, 0), (0, LANES - MLA_NOPE))).reshape(
        n_c, MLA_KV_RANK, -1)
    wv = wkv[..., MLA_NOPE:].reshape(n_c, MLA_KV_RANK, -1)
    w_ukv_pad_all = jnp.concatenate([wk, wv], axis=2).astype(BF16)
    cache_kr_all = jnp.pad(cache_mla_krope, ((0, 0), (0, 0), (0, 0), (MLA_NOPE, LANES - MLA_NOPE - MLA_ROPE)))

    ret_l, ssm_re_l, ssm_im_l, ckv_l, kr_l = [], [], [], [], []
    mix = jnp.zeros((N_TOK, D_MODEL), BF16)
    for l in range(DEPTH):
        i = l // 2
        mod = mods[l]
        if l % 2 == 0:
            qkvg, u = _pre_ab(x, norm1_g[l], mod, w_in_ab_bf[i])
            w1, w2, a16 = s5_w1[i], s5_w2[i], s5_a16[i]
            x0_p = jnp.zeros((SSM_GROUPS, 2, BATCH, LANES), F32)
            x0_s = jnp.stack(
                [state_ssm_re[:, i].transpose(2, 0, 1, 3).reshape(SSM_GROUPS, DEC_BATCH, LANES),
                 state_ssm_im[:, i].transpose(2, 0, 1, 3).reshape(SSM_GROUPS, DEC_BATCH, LANES)], axis=1)
            yp, xf_p = _s5(u, w1, w2, a16, x0_p, BATCH, SEQ // SSM_CHUNK, 0)
            ys, _ = _s5(u, w1, w2, a16, x0_s, DEC_BATCH, DEC_SEQ // SSM_CHUNK, 1)
            dsum, dec, cd = ret_dsum[i], ret_dec[i], ret_cd[i]
            mix, s_ret = _ab_core(qkvg, u, yp, dsum, dec, cd, ssm_d[i],
                                  w_glu_bf[i], mix, nb=BATCH, seq=SEQ, row0=0)
            (mix,) = _ab_core(qkvg, u, ys, dsum, dec, cd, ssm_d[i],
                              w_glu_bf[i], mix, nb=DEC_BATCH, seq=DEC_SEQ, row0=NP,
                              rope=ret_rope, s0=state_ret[:, i])
            ret_l.append(s_ret)
            xf = xf_p.reshape(SSM_GROUPS, 2, BATCH, 2, SSM_STATE).transpose(1, 2, 3, 0, 4)
            ssm_re_l.append(xf[0])
            ssm_im_l.append(xf[1])
        else:
            w_ukv_pad = w_ukv_pad_all[i]
            q, ckv, kr, krp = _pre_c(x, norm1_g[l], mod, w_in_c_pad[i], mla_q_norm_g[i], mla_kv_norm_g[i],
                                     w_uq_pad_all[i], mla_cos, mla_sin)
            ckv_p = ckv[:NP].reshape(BATCH, SEQ, MLA_KV_RANK)
            kr_p = kr[:NP].reshape(BATCH, SEQ, LANES)
            ckv_s = jnp.concatenate([cache_mla_ckv[:, i], ckv[NP:].reshape(DEC_BATCH, DEC_SEQ, MLA_KV_RANK)],
                                    axis=1)
            kr_s = jnp.concatenate([cache_kr_all[:, i], krp[NP:].reshape(DEC_BATCH, DEC_SEQ, LANES)], axis=1)
            mix = _attention(q, ckv_p, kr_p, w_ukv_pad, mix, nb=BATCH, seq=SEQ, lk=SEQ, row0=0)
            mix = _attention(q, ckv_s, kr_s, w_ukv_pad, mix, nb=DEC_BATCH, seq=DEC_SEQ,
                             lk=PAST_LEN + DEC_SEQ, row0=NP)
            ckv_l.append(ckv_p)
            kr_l.append(kr_p[:, :, MLA_NOPE:MLA_NOPE + MLA_ROPE])
        xmid, h2, route, counts = _post(mix, x, mod, norm2_g[l], w_out_bf[l], router_w[l], router_b[l])
        dest, blk_e, n_used, zero_blk = _route_plan(route, counts)
        dest3 = dest.reshape(N_TOK // TMC, 1, TMC * TOP_K)
        xs = _dispatch(h2, dest3, zero_blk)
        y = _moe(xs, blk_e, n_used, w_in_all, b_in_all, w_out_all, b_out_all, l)
        x = _combine(y, dest3, xmid, mod, route, final_norm_g, final=(l == DEPTH - 1))

    y_prompt = x[:NP].reshape(BATCH, SEQ, D_MODEL)
    y_sample = x[NP:].reshape(DEC_BATCH, DEC_SEQ, D_MODEL)
    return (y_prompt, y_sample, jnp.stack(ret_l, axis=1), jnp.stack(ssm_re_l, axis=1),
            jnp.stack(ssm_im_l, axis=1), jnp.stack(ckv_l, axis=1), jnp.stack(kr_l, axis=1))
```

```python
import functools
import math

import jax
import jax.numpy as jnp
import numpy as np
from jax import lax
from jax.experimental import pallas as pl
from jax.experimental.pallas import tpu as pltpu

F32 = jnp.float32
BF16 = jnp.bfloat16
HIGHEST = lax.Precision.HIGHEST

D_MODEL = 1024
BATCH = 32
SEQ = 256
DEPTH = 4
DEC_BATCH = 8
DEC_SEQ = 1024
PAST_LEN = 512
GRID_W = 64
RET_HEADS = 4
RET_V_DIM = 128
RET_QK_DIM = 64
RET_CHUNK = 128
SSM_WIDTH = 512
SSM_GROUP = 16
SSM_GROUPS = 32
SSM_STATE = 64
SSM_CHUNK = 16
MLA_HEADS = 8
MLA_NOPE = 64
MLA_ROPE = 32
MLA_V = 128
MLA_Q_RANK = 256
MLA_KV_RANK = 128
N_EXPERTS = 32
TOP_K = 4
D_EXPERT = 1024
SWIGLU_LIMIT = 7.0
SWIGLU_ALPHA = 1.702
ROPE_BASE = 10000.0
NORM_EPS = 1e-6

NP = BATCH * SEQ
NS = DEC_BATCH * DEC_SEQ
N_TOK = NP + NS
N_SEG = 1 + DEC_BATCH
LANES = 128

TM = 512
TMC = 256
TME = 512
N_ASSIGN = N_TOK * TOP_K
N_EBLK = N_ASSIGN // TME + N_EXPERTS
P_ROWS = N_EBLK * TME
VMEM_LIMIT = 56 * 1024 * 1024


def _params(sem, vmem=VMEM_LIMIT):
    return pltpu.CompilerParams(dimension_semantics=sem, vmem_limit_bytes=vmem)


def _bdot(a, b):
    return jnp.dot(a.astype(BF16), b.astype(BF16), preferred_element_type=F32)


def _split(x):
    hi = x.astype(BF16)
    lo = (x - hi.astype(F32)).astype(BF16)
    return hi, lo


def _dot3(a, b):
    ah, al = _split(a)
    bh, bl = _split(b)
    d = functools.partial(jnp.dot, preferred_element_type=F32)
    return d(ah, bh) + d(al, bh) + d(ah, bl)


def _rms(x, g):
    return x * lax.rsqrt(jnp.mean(x * x, axis=-1, keepdims=True) + NORM_EPS) * g


SLAB = D_MODEL // LANES


def _store_slabs(ref, row0, val):
    m = val.shape[0]
    for s in range(SLAB):
        ref[pl.ds(row0 * SLAB + s, m, stride=SLAB), :] = val[:, s * LANES:(s + 1) * LANES]


def _load_slabs(ref, row0, m):
    return jnp.concatenate([ref[pl.ds(row0 * SLAB + s, m, stride=SLAB), :] for s in range(SLAB)], axis=-1)


def _slab_copy(src, dst, sem, src_row, dst_row):
    return pltpu.make_async_copy(src.at[pl.ds(pl.multiple_of(src_row * SLAB, SLAB), SLAB)],
                                 dst.at[pl.ds(pl.multiple_of(dst_row * SLAB, SLAB), SLAB)], sem)


def _seg(i, tm):
    return jnp.where(i < NP // tm, 0, 1 + (i * tm - NP) // DEC_SEQ)


def _mod_kernel(c_ref, w_ref, b_ref, o_ref):
    c = c_ref[...]
    s = c * jax.nn.sigmoid(c)
    o_ref[0] = _dot3(s, w_ref[0]) + b_ref[0]


def _modulation(c_ctx, c, w_ada, b_ada):
    rows = 16
    cvec = jnp.zeros((rows, D_MODEL), F32).at[0].set(c_ctx).at[1:N_SEG].set(c)
    nb = 1536
    out = pl.pallas_call(
        _mod_kernel,
        grid=(DEPTH, 6 * D_MODEL // nb),
        in_specs=[pl.BlockSpec((rows, D_MODEL), lambda l, j: (0, 0)),
                  pl.BlockSpec((1, D_MODEL, nb), lambda l, j: (l, 0, j)),
                  pl.BlockSpec((1, 1, nb), lambda l, j: (l, 0, j))],
        out_specs=pl.BlockSpec((1, rows, nb), lambda l, j: (l, 0, j)),
        out_shape=jax.ShapeDtypeStruct((DEPTH, rows, 6 * D_MODEL), F32),
        compiler_params=_params(("arbitrary", "arbitrary")),
        name="adaln_mod",
    )(cvec, w_ada, b_ada.reshape(DEPTH, 1, 6 * D_MODEL))
    return out[:, :N_SEG].reshape(DEPTH, N_SEG, 6, D_MODEL)


def _pre_ab_kernel(x_ref, g_ref, mod_ref, w_ref, qkvg_ref, u_ref):
    m = mod_ref[0]
    h = _rms(x_ref[...], g_ref[...]) * (1.0 + m[1:2]) + m[0:1]
    p = _bdot(h, w_ref[...])
    qkvg_ref[...] = p[:, :1536]
    for k in range(SSM_WIDTH // LANES):
        u_ref[k] = p[:, 1536 + k * LANES:1536 + (k + 1) * LANES]


def _pre_ab(x, g1, mod, w_in):
    return pl.pallas_call(
        _pre_ab_kernel,
        grid=(N_TOK // TM,),
        in_specs=[pl.BlockSpec((TM, D_MODEL), lambda i: (i, 0)),
                  pl.BlockSpec((1, D_MODEL), lambda i: (0, 0)),
                  pl.BlockSpec((1, 6, D_MODEL), lambda i: (_seg(i, TM), 0, 0)),
                  pl.BlockSpec((D_MODEL, 2048), lambda i: (0, 0))],
        out_specs=[pl.BlockSpec((TM, 1536), lambda i: (i, 0)),
                   pl.BlockSpec((SSM_WIDTH // LANES, TM, LANES), lambda i: (0, i, 0))],
        out_shape=[jax.ShapeDtypeStruct((N_TOK, 1536), F32),
                   jax.ShapeDtypeStruct((SSM_WIDTH // LANES, N_TOK, LANES), F32)],
        compiler_params=_params(("arbitrary",)),
        name="pre_ab",
    )(x, g1.reshape(1, D_MODEL), mod, w_in)


def _rope_lanes(x, cos, sins, first, shift):
    w = x.shape[-1]
    partner = jnp.where(first, pltpu.roll(x, w - shift, 1), pltpu.roll(x, shift, 1))
    return x * cos + partner * sins


def _pre_c_kernel(x_ref, g_ref, mod_ref, w_ref, qg_ref, kvg_ref, wuq_ref, cos_ref, sin_ref,
                  q_ref, ckv_ref, kr_ref, krp_ref):
    m = mod_ref[0]
    h = _rms(x_ref[...], g_ref[...]) * (1.0 + m[1:2]) + m[0:1]
    p = _bdot(h, w_ref[...])
    cq = _rms(p[:, :MLA_Q_RANK], qg_ref[...])
    ckv_ref[...] = _rms(p[:, MLA_Q_RANK:MLA_Q_RANK + MLA_KV_RANK], kvg_ref[...])
    kr = p[:, MLA_Q_RANK + MLA_KV_RANK:]
    kr_ref[...] = kr
    cos = cos_ref[...]
    sins = sin_ref[...]
    lane = lax.broadcasted_iota(jnp.int32, cos.shape, 1)
    first = (lane % 16) < 8
    krp_ref[...] = _rope_lanes(kr, cos, sins, first, 8)
    q = _bdot(cq, wuq_ref[...])
    for hh in range(MLA_HEADS):
        sl = slice(hh * LANES, (hh + 1) * LANES)
        q_ref[:, sl] = _rope_lanes(q[:, sl], cos, sins, first, 8).astype(BF16)


def _pre_c(x, g1, mod, w_in_pad, q_g, kv_g, w_uq_pad, cos_t, sin_t):
    def tab(i):
        return (jnp.where(i < NP // TM, 0, 1 + ((i * TM - NP) % DEC_SEQ) // TM), 0)
    return pl.pallas_call(
        _pre_c_kernel,
        grid=(N_TOK // TM,),
        in_specs=[pl.BlockSpec((TM, D_MODEL), lambda i: (i, 0)),
                  pl.BlockSpec((1, D_MODEL), lambda i: (0, 0)),
                  pl.BlockSpec((1, 6, D_MODEL), lambda i: (_seg(i, TM), 0, 0)),
                  pl.BlockSpec((D_MODEL, 512), lambda i: (0, 0)),
                  pl.BlockSpec((1, MLA_Q_RANK), lambda i: (0, 0)),
                  pl.BlockSpec((1, MLA_KV_RANK), lambda i: (0, 0)),
                  pl.BlockSpec((MLA_Q_RANK, MLA_HEADS * LANES), lambda i: (0, 0)),
                  pl.BlockSpec((TM, LANES), tab),
                  pl.BlockSpec((TM, LANES), tab)],
        out_specs=[pl.BlockSpec((TM, MLA_HEADS * LANES), lambda i: (i, 0)),
                   pl.BlockSpec((TM, MLA_KV_RANK), lambda i: (i, 0)),
                   pl.BlockSpec((TM, LANES), lambda i: (i, 0)),
                   pl.BlockSpec((TM, LANES), lambda i: (i, 0))],
        out_shape=[jax.ShapeDtypeStruct((N_TOK, MLA_HEADS * LANES), BF16),
                   jax.ShapeDtypeStruct((N_TOK, MLA_KV_RANK), F32),
                   jax.ShapeDtypeStruct((N_TOK, LANES), F32),
                   jax.ShapeDtypeStruct((N_TOK, LANES), F32)],
        compiler_params=_params(("arbitrary",)),
        name="pre_c",
    )(x, g1.reshape(1, D_MODEL), mod, w_in_pad, q_g.reshape(1, -1), kv_g.reshape(1, -1), w_uq_pad,
      cos_t, sin_t)


def _s5_tables(a_re, a_im, log_dt, b_re, b_im, c_re, c_im):
    C = SSM_CHUNK
    a_re = jnp.minimum(a_re.astype(F32), -1e-4)
    a_im = a_im.astype(F32)
    dt = jnp.exp(log_dt.astype(F32))[..., None]
    mag = jnp.exp(dt * a_re)
    ab_re = mag * jnp.cos(dt * a_im)
    ab_im = mag * jnp.sin(dt * a_im)
    den = a_re * a_re + a_im * a_im
    f_re = ((ab_re - 1.0) * a_re + ab_im * a_im) / den
    f_im = (ab_im * a_re - (ab_re - 1.0) * a_im) / den
    b_re = b_re.astype(F32)
    b_im = b_im.astype(F32)
    bb_re = f_re[..., None] * b_re - f_im[..., None] * b_im
    bb_im = f_re[..., None] * b_im + f_im[..., None] * b_re
    ks = jnp.arange(C + 1, dtype=F32)[:, None, None, None]
    pmag = jnp.exp(ks * (dt * a_re)[None])
    p_re = pmag * jnp.cos(ks * (dt * a_im)[None])
    p_im = pmag * jnp.sin(ks * (dt * a_im)[None])
    c_re = c_re.astype(F32)[None]
    c_im = c_im.astype(F32)[None]
    cl_re = c_re * p_re[:, :, :, None, :] - c_im * p_im[:, :, :, None, :]
    cl_im = c_re * p_im[:, :, :, None, :] + c_im * p_re[:, :, :, None, :]
    kmat = (jnp.einsum('kdgpn,dgnq->kdgpq', cl_re[:C], bb_re, precision=HIGHEST)
            - jnp.einsum('kdgpn,dgnq->kdgpq', cl_im[:C], bb_im, precision=HIGHEST))
    lag = np.arange(C)[None, :] - np.arange(C)[:, None]
    sel_f = (lag[:, :, None] == np.arange(C)).reshape(C * C, C).astype(np.float32)
    sel_b = (-lag[:, :, None] == np.arange(C)).reshape(C * C, C).astype(np.float32)
    t = (jnp.dot(sel_f, kmat[:, 0].reshape(C, -1), precision=HIGHEST)
         + jnp.dot(sel_b, kmat[:, 1].reshape(C, -1), precision=HIGHEST))
    m_loc = (t.reshape(C, C, SSM_GROUPS, SSM_GROUP, SSM_GROUP).transpose(2, 0, 4, 1, 3)
             .reshape(SSM_GROUPS, C * SSM_GROUP, C * SSM_GROUP))

    def state_in(d, flip):
        pr = (p_re[:C, d][::-1] if flip else p_re[:C, d])[..., None]
        pi = (p_im[:C, d][::-1] if flip else p_im[:C, d])[..., None]
        er = pr * bb_re[d][None] - pi * bb_im[d][None]
        ei = pr * bb_im[d][None] + pi * bb_re[d][None]
        to = lambda e: e.transpose(1, 0, 3, 2).reshape(SSM_GROUPS, C * SSM_GROUP, SSM_STATE)
        return to(er), to(ei)

    ef_re, ef_im = state_in(0, True)
    eb_re, eb_im = state_in(1, False)
    w1 = jnp.concatenate([m_loc, ef_re, eb_re, ef_im, eb_im], axis=-1)

    def state_out(d, flip):
        to = lambda e: e.transpose(1, 3, 0, 2).reshape(SSM_GROUPS, SSM_STATE, C * SSM_GROUP)
        cr = cl_re[1:C + 1, d][::-1] if flip else cl_re[1:C + 1, d]
        ci = cl_im[1:C + 1, d][::-1] if flip else cl_im[1:C + 1, d]
        return to(cr), to(-ci)

    ff_re, ff_im = state_out(0, False)
    fb_re, fb_im = state_out(1, True)
    w2 = jnp.concatenate([ff_re, fb_re, ff_im, fb_im], axis=1)
    a16 = jnp.stack([jnp.concatenate([p_re[C, 0], p_re[C, 1]], axis=-1),
                     jnp.concatenate([p_im[C, 0], p_im[C, 1]], axis=-1)], axis=1)
    return w1.astype(BF16), w2.astype(BF16), a16


S5_GB = LANES // SSM_GROUP


def _s5_kernel(nb, nc, u_ref, w1_ref, w2_ref, a_ref, x0_ref, y_ref, xf_ref,
               ug_ref, yl_ref, lre_ref, lim_ref, s_ref, l2re_ref, l2im_ref):
    r = nb * nc
    for s in range(SSM_CHUNK):
        piece = u_ref[pl.ds(s, r, stride=SSM_CHUNK), :]
        for gl in range(S5_GB):
            ug_ref[gl, :, s * SSM_GROUP:(s + 1) * SSM_GROUP] = piece[:, gl * SSM_GROUP:(gl + 1) * SSM_GROUP]
    fwd = lax.broadcasted_iota(jnp.int32, (nb, LANES), 1) < SSM_STATE

    def group(gl, carry):
        r1 = _bdot(ug_ref[gl], w1_ref[gl])
        yl_ref[gl] = r1[:, 0:256]
        lre_ref[...] = r1[:, 256:384]
        lim_ref[...] = r1[:, 384:512]
        a = a_ref[gl]
        a_re = a[0:1]
        a_im = a[1:2]
        s_re = x0_ref[gl, 0]
        s_im = x0_ref[gl, 1]
        s_ref[0, 0] = s_re
        s_ref[1, 0] = s_im
        for k in range(nc):
            rf = pl.ds(k, nb, stride=nc)
            rb = pl.ds(nc - 1 - k, nb, stride=nc)
            l_re = jnp.where(fwd, lre_ref[rf, :], lre_ref[rb, :])
            l_im = jnp.where(fwd, lim_ref[rf, :], lim_ref[rb, :])
            n_re = a_re * s_re - a_im * s_im + l_re
            n_im = a_re * s_im + a_im * s_re + l_im
            s_re, s_im = n_re, n_im
            s_ref[0, k + 1] = s_re
            s_ref[1, k + 1] = s_im
        xf_ref[gl, 0] = s_re
        xf_ref[gl, 1] = s_im
        for c in range(nc):
            rows = pl.ds(c, nb, stride=nc)
            l2re_ref[rows, :] = jnp.where(fwd, s_ref[0, c], s_ref[0, nc - 1 - c])
            l2im_ref[rows, :] = jnp.where(fwd, s_ref[1, c], s_ref[1, nc - 1 - c])
        l2 = jnp.concatenate([l2re_ref[...], l2im_ref[...]], axis=-1)
        yl_ref[gl] = yl_ref[gl] + _bdot(l2, w2_ref[gl])
        return carry

    lax.fori_loop(0, S5_GB, group, 0)
    for j in range(SSM_CHUNK):
        tile = jnp.concatenate([yl_ref[gl, :, j * SSM_GROUP:(j + 1) * SSM_GROUP] for gl in range(S5_GB)],
                               axis=-1)
        y_ref[pl.ds(j, r, stride=SSM_CHUNK), :] = tile


def _s5(u4, w1, w2, a16, x0, nb, nc, grp):
    r = nb * nc
    t = r * SSM_CHUNK
    nblk = SSM_WIDTH // LANES
    return pl.pallas_call(
        functools.partial(_s5_kernel, nb, nc),
        grid=(nblk,),
        in_specs=[pl.BlockSpec((None, t, LANES), lambda cb: (cb, grp, 0)),
                  pl.BlockSpec((S5_GB, 256, 512), lambda cb: (cb, 0, 0)),
                  pl.BlockSpec((S5_GB, 256, 256), lambda cb: (cb, 0, 0)),
                  pl.BlockSpec((S5_GB, 2, LANES), lambda cb: (cb, 0, 0)),
                  pl.BlockSpec((S5_GB, 2, nb, LANES), lambda cb: (cb, 0, 0, 0))],
        out_specs=[pl.BlockSpec((None, t, LANES), lambda cb: (cb, 0, 0)),
                   pl.BlockSpec((S5_GB, 2, nb, LANES), lambda cb: (cb, 0, 0, 0))],
        out_shape=[jax.ShapeDtypeStruct((nblk, t, LANES), F32),
                   jax.ShapeDtypeStruct((SSM_GROUPS, 2, nb, LANES), F32)],
        scratch_shapes=[pltpu.VMEM((S5_GB, r, 256), F32),
                        pltpu.VMEM((S5_GB, r, 256), F32),
                        pltpu.VMEM((r, LANES), F32),
                        pltpu.VMEM((r, LANES), F32),
                        pltpu.VMEM((2, nc + 1, nb, LANES), F32),
                        pltpu.VMEM((r, LANES), F32),
                        pltpu.VMEM((r, LANES), F32)],
        compiler_params=_params(("arbitrary",)),
        name="s5_chunked",
    )(u4, w1, w2, a16, x0)


def _ret_tables(logit):
    C = RET_CHUNK
    lg = jax.nn.log_sigmoid(logit.astype(F32))
    pos = jnp.arange(C, dtype=F32)
    diff = pos[:, None] - pos[None, :]
    dsum = (jnp.where(diff >= 0, jnp.exp(jnp.maximum(diff, 0.0)[None] * lg[0][:, None, None]), 0.0)
            + jnp.where(diff <= 0, jnp.exp(jnp.maximum(-diff, 0.0)[None] * lg[1][:, None, None]), 0.0))
    kdf = jnp.exp((C - 1 - pos)[:, None] * lg[0][None, :])
    kdb = jnp.exp(pos[:, None] * lg[1][None, :])
    qdf = jnp.exp((pos + 1)[:, None] * lg[0][None, :])
    qdb = jnp.exp((C - pos)[:, None] * lg[1][None, :])
    dec = jnp.stack([jnp.repeat(t, RET_QK_DIM, axis=1) for t in (kdf, kdb, qdf, qdb)])
    cd = jnp.exp(C * lg)
    return dsum, dec, cd


def _ab_core_kernel(seq, use_pos, with_state, *refs):
    it = iter(refs)
    qkvg_ref, u_ref, ycv_ref, dsum_ref, dec_ref, cd_ref, d_ref, wglu_ref = [next(it) for _ in range(8)]
    if use_pos:
        cos_ref, sin_ref = next(it), next(it)
    s0_ref = next(it) if use_pos else None
    next(it)
    mix_ref = next(it)
    st_ref = next(it) if with_state else None
    C = RET_CHUNK
    n = seq // C
    q = qkvg_ref[:, 0:256] * (RET_QK_DIM ** -0.5)
    k = qkvg_ref[:, 256:512]
    if use_pos:
        cos = cos_ref[...]
        sins = sin_ref[...]
        first = (lax.broadcasted_iota(jnp.int32, cos.shape, 1) % 32) < 16
        q = _rope_lanes(q, cos, sins, first, 16)
        k = _rope_lanes(k, cos, sins, first, 16)
    kdf, kdb, qdf, qdb = dec_ref[0], dec_ref[1], dec_ref[2], dec_ref[3]
    for h in range(RET_HEADS):
        qs = slice(h * RET_QK_DIM, (h + 1) * RET_QK_DIM)
        vs = slice(512 + h * RET_V_DIM, 512 + (h + 1) * RET_V_DIM)
        gs = slice(1024 + h * RET_V_DIM, 1024 + (h + 1) * RET_V_DIM)
        qc, kc, vc, kvf, kvb = [], [], [], [], []
        for c in range(n):
            rows = slice(c * C, (c + 1) * C)
            qc.append(q[rows, qs])
            kc.append(k[rows, qs])
            vc.append(qkvg_ref[rows, vs])
            kvf.append(_bdot((kc[c] * kdf[:, qs]).T, vc[c]))
            kvb.append(_bdot((kc[c] * kdb[:, qs]).T, vc[c]))
        if use_pos:
            sf = s0_ref[0, 0, h]
            sb = s0_ref[0, 1, h]
        else:
            sf = jnp.zeros((RET_QK_DIM, RET_V_DIM), F32)
            sb = jnp.zeros((RET_QK_DIM, RET_V_DIM), F32)
        prev_f = []
        for c in range(n):
            prev_f.append(sf)
            sf = cd_ref[0, h] * sf + kvf[c]
        next_b = [None] * n
        for c in range(n - 1, -1, -1):
            next_b[c] = sb
            sb = cd_ref[1, h] * sb + kvb[c]
        if with_state:
            st_ref[0, 0, h] = sf
            st_ref[0, 1, h] = sb
        for c in range(n):
            rows = slice(c * C, (c + 1) * C)
            sc = lax.dot_general(qc[c].astype(BF16), kc[c].astype(BF16), (((1,), (1,)), ((), ())),
                                 preferred_element_type=F32) * dsum_ref[h]
            o = (_bdot(sc, vc[c]) + _bdot(qc[c] * qdf[:, qs], prev_f[c])
                 + _bdot(qc[c] * qdb[:, qs], next_b[c]))
            o = o * lax.rsqrt(jnp.mean(o * o, axis=-1, keepdims=True) + NORM_EPS)
            g = qkvg_ref[rows, gs]
            mix_ref[rows, h * RET_V_DIM:(h + 1) * RET_V_DIM] = (o * (g * jax.nn.sigmoid(g))).astype(BF16)
    nblk = SSM_WIDTH // LANES
    u = jnp.concatenate([u_ref[kb] for kb in range(nblk)], axis=-1)
    ycv = jnp.concatenate([ycv_ref[kb] for kb in range(nblk)], axis=-1)
    y = ycv + d_ref[...] * u
    z = 0.5 * y * (1.0 + jnp.tanh(math.sqrt(2.0 / math.pi) * (y + 0.044715 * (y * y * y))))
    z = z * jax.nn.sigmoid(_bdot(z, wglu_ref[...]))
    mix_ref[:, 512:1024] = z.astype(BF16)


def _ab_core(qkvg, u, ycv, dsum, dec, cd, d, w_glu, mix_prev, *, nb, seq, row0, rope=None, s0=None):
    use_pos = rope is not None
    with_state = not use_pos
    blk0 = row0 // seq
    in_specs = [pl.BlockSpec((seq, 1536), lambda b: (blk0 + b, 0)),
                pl.BlockSpec((SSM_WIDTH // LANES, seq, LANES), lambda b: (0, blk0 + b, 0)),
                pl.BlockSpec((SSM_WIDTH // LANES, seq, LANES), lambda b: (0, b, 0)),
                pl.BlockSpec((RET_HEADS, RET_CHUNK, RET_CHUNK), lambda b: (0, 0, 0)),
                pl.BlockSpec((4, RET_CHUNK, 256), lambda b: (0, 0, 0)),
                pl.BlockSpec(memory_space=pltpu.SMEM),
                pl.BlockSpec((1, SSM_WIDTH), lambda b: (0, 0)),
                pl.BlockSpec((SSM_WIDTH, SSM_WIDTH), lambda b: (0, 0))]
    args = [qkvg, u, ycv, dsum, dec, cd, d.reshape(1, SSM_WIDTH), w_glu]
    if use_pos:
        in_specs += [pl.BlockSpec((seq, 256), lambda b: (0, 0)),
                     pl.BlockSpec((seq, 256), lambda b: (0, 0)),
                     pl.BlockSpec((1, 2, RET_HEADS, RET_QK_DIM, RET_V_DIM), lambda b: (b, 0, 0, 0, 0))]
        args += [rope[0], rope[1], s0]
    in_specs.append(pl.BlockSpec(memory_space=pl.ANY))
    args.append(mix_prev)
    out_specs = [pl.BlockSpec((seq, D_MODEL), lambda b: (blk0 + b, 0))]
    out_shape = [jax.ShapeDtypeStruct((N_TOK, D_MODEL), BF16)]
    if with_state:
        out_specs.append(pl.BlockSpec((1, 2, RET_HEADS, RET_QK_DIM, RET_V_DIM), lambda b: (b, 0, 0, 0, 0)))
        out_shape.append(jax.ShapeDtypeStruct((nb, 2, RET_HEADS, RET_QK_DIM, RET_V_DIM), F32))
    return pl.pallas_call(
        functools.partial(_ab_core_kernel, seq, use_pos, with_state),
        grid=(nb,),
        in_specs=in_specs,
        out_specs=out_specs,
        out_shape=out_shape,
        input_output_aliases={len(args) - 1: 0},
        compiler_params=_params(("arbitrary",)),
        name="ab_core_pos" if use_pos else "ab_core",
    )(*args)


def _attn_kernel(lk, tq, q_ref, ckv_ref, kr_ref, wukv_ref, _, o_ref, k_sc, v_sc):
    @pl.when(pl.program_id(1) == 0)
    def _():
        kv = _bdot(ckv_ref[0], wukv_ref[...])
        kr = kr_ref[0]
        scale = (MLA_NOPE + MLA_ROPE) ** -0.5
        for h in range(MLA_HEADS):
            sl = slice(h * LANES, (h + 1) * LANES)
            k_sc[:, sl] = ((kv[:, sl] + kr) * scale).astype(BF16)
        v_sc[...] = kv[:, MLA_HEADS * LANES:].astype(BF16)

    for h in range(MLA_HEADS):
        sl = slice(h * LANES, (h + 1) * LANES)
        s = lax.dot_general(q_ref[:, sl], k_sc[:, sl], (((1,), (1,)), ((), ())),
                            preferred_element_type=F32)
        e = jnp.exp(s - jnp.max(s, axis=-1, keepdims=True))
        o = _bdot(e, v_sc[:, sl]) / jnp.sum(e, axis=-1, keepdims=True)
        o_ref[:, sl] = o.astype(BF16)


def _attention(q, ckv_all, kr_all, w_ukv_pad, o_prev, *, nb, seq, lk, row0, tq=256):
    nq = seq // tq
    blk0 = row0 // tq
    return pl.pallas_call(
        functools.partial(_attn_kernel, lk, tq),
        grid=(nb, nq),
        in_specs=[pl.BlockSpec((tq, MLA_HEADS * LANES), lambda b, i: (blk0 + b * nq + i, 0)),
                  pl.BlockSpec((1, lk, MLA_KV_RANK), lambda b, i: (b, 0, 0)),
                  pl.BlockSpec((1, lk, LANES), lambda b, i: (b, 0, 0)),
                  pl.BlockSpec((MLA_KV_RANK, 2 * MLA_HEADS * LANES), lambda b, i: (0, 0)),
                  pl.BlockSpec(memory_space=pl.ANY)],
        out_specs=pl.BlockSpec((tq, MLA_HEADS * MLA_V), lambda b, i: (blk0 + b * nq + i, 0)),
        out_shape=jax.ShapeDtypeStruct((N_TOK, MLA_HEADS * MLA_V), BF16),
        scratch_shapes=[pltpu.VMEM((lk, MLA_HEADS * LANES), BF16),
                        pltpu.VMEM((lk, MLA_HEADS * MLA_V), BF16)],
        input_output_aliases={4: 0},
        compiler_params=_params(("arbitrary", "arbitrary")),
        name="mla_attn_%d" % lk,
    )(q, ckv_all, kr_all, w_ukv_pad, o_prev)


def _post_kernel(mix_ref, x_ref, mod_ref, g_ref, wo_ref, rw_ref, rb_ref,
                 xmid_ref, h2_ref, route_ref, cnt_ref, base_ref):
    i = pl.program_id(0)

    @pl.when(i == 0)
    def _():
        base_ref[...] = jnp.zeros_like(base_ref)

    m = mod_ref[0]
    xm = x_ref[...] + m[2:3] * jnp.dot(mix_ref[...], wo_ref[...], preferred_element_type=F32)
    xmid_ref[...] = xm
    h2 = _rms(xm, g_ref[...]) * (1.0 + m[4:5]) + m[3:4]
    _store_slabs(h2_ref, 0, h2)
    logits = _dot3(h2, rw_ref[...]) + rb_ref[...]
    lane_f = lax.broadcasted_iota(jnp.int32, logits.shape, 1).astype(F32)
    l = logits
    vals, sels = [], []
    for _k in range(TOP_K):
        mx = jnp.max(l, axis=-1, keepdims=True)
        idx = jnp.min(jnp.where(l == mx, lane_f, float(N_EXPERTS)), axis=-1, keepdims=True)
        sel = lane_f == idx
        vals.append(mx)
        sels.append(sel)
        l = jnp.where(sel, -jnp.inf, l)
    ex = [jnp.exp(v - vals[0]) for v in vals]
    den = ex[0] + ex[1] + ex[2] + ex[3]
    onehot = (sels[0] | sels[1] | sels[2] | sels[3]).astype(F32)
    r_i = lax.broadcasted_iota(jnp.int32, (TM, TM), 0)
    c_i = lax.broadcasted_iota(jnp.int32, (TM, TM), 1)
    tri = (c_i < r_i).astype(BF16)
    before = base_ref[...] + jnp.dot(tri, onehot.astype(BF16), preferred_element_type=F32)
    out_lane = lax.broadcasted_iota(jnp.int32, (TM, LANES), 1)
    route = jnp.zeros((TM, LANES), F32)
    for kk in range(TOP_K):
        sel_f = sels[kk].astype(F32)
        eid = jnp.sum(sel_f * lane_f, axis=-1, keepdims=True)
        pos = jnp.sum(sel_f * before, axis=-1, keepdims=True)
        route = jnp.where(out_lane == kk, eid, route)
        route = jnp.where(out_lane == TOP_K + kk, ex[kk] / den, route)
        route = jnp.where(out_lane == 2 * TOP_K + kk, pos, route)
    route_ref[...] = route
    base_ref[...] = base_ref[...] + jnp.sum(onehot, axis=0, keepdims=True)
    cnt_ref[...] = base_ref[...]


def _post(mix, x, mod, g2, w_out, router_w, router_b):
    return pl.pallas_call(
        _post_kernel,
        grid=(N_TOK // TM,),
        in_specs=[pl.BlockSpec((TM, D_MODEL), lambda i: (i, 0)),
                  pl.BlockSpec((TM, D_MODEL), lambda i: (i, 0)),
                  pl.BlockSpec((1, 6, D_MODEL), lambda i: (_seg(i, TM), 0, 0)),
                  pl.BlockSpec((1, D_MODEL), lambda i: (0, 0)),
                  pl.BlockSpec((D_MODEL, D_MODEL), lambda i: (0, 0)),
                  pl.BlockSpec((D_MODEL, N_EXPERTS), lambda i: (0, 0)),
                  pl.BlockSpec((1, N_EXPERTS), lambda i: (0, 0))],
        out_specs=[pl.BlockSpec((TM, D_MODEL), lambda i: (i, 0)),
                   pl.BlockSpec((TM * SLAB, LANES), lambda i: (i, 0)),
                   pl.BlockSpec((TM, LANES), lambda i: (i, 0)),
                   pl.BlockSpec((1, N_EXPERTS), lambda i: (0, 0))],
        out_shape=[jax.ShapeDtypeStruct((N_TOK, D_MODEL), F32),
                   jax.ShapeDtypeStruct((N_TOK * SLAB, LANES), F32),
                   jax.ShapeDtypeStruct((N_TOK, LANES), F32),
                   jax.ShapeDtypeStruct((1, N_EXPERTS), F32)],
        scratch_shapes=[pltpu.VMEM((1, N_EXPERTS), F32)],
        compiler_params=_params(("arbitrary",)),
        name="post_router",
    )(mix, x, mod, g2.reshape(1, D_MODEL), w_out, router_w, router_b.reshape(1, N_EXPERTS))


def _route_plan(route, counts):
    eid = route[:, 0:TOP_K].astype(jnp.int32)
    pos = route[:, 2 * TOP_K:3 * TOP_K].astype(jnp.int32)
    cnt = counts[0].astype(jnp.int32)
    blocks = (cnt + TME - 1) // TME
    blk_end = jnp.cumsum(blocks)
    pad_start = (blk_end - blocks) * TME
    experts = jnp.arange(N_EXPERTS, dtype=jnp.int32)
    dest = jnp.sum(jnp.where(eid[:, :, None] == experts, pad_start, 0), axis=-1) + pos
    n_used = blk_end[-1]
    blk = jnp.arange(N_EBLK, dtype=jnp.int32)
    blk_e = jnp.sum((jnp.minimum(blk, n_used - 1)[:, None] >= blk_end[None, :]).astype(jnp.int32), axis=-1)
    blk_e = jnp.minimum(blk_e, N_EXPERTS - 1).astype(jnp.int32)
    is_last = jnp.any((blk[:, None] == blk_end[None, :] - 1) & (blocks[None, :] > 0), axis=-1)
    zero_blk = (is_last | (blk >= n_used)).astype(jnp.int32)
    return dest.astype(jnp.int32), blk_e, n_used.reshape(1).astype(jnp.int32), zero_blk


def _dispatch_kernel(zero_ref, dest_ref, h2_ref, xs_hbm, zbuf, sem, zsem):
    i = pl.program_id(0)

    @pl.when(i == 0)
    def _():
        zbuf[...] = jnp.zeros_like(zbuf)

        def zero_fill(wait):
            def body(j, carry):
                @pl.when(zero_ref[j] != 0)
                def _():
                    cp = pltpu.make_async_copy(
                        zbuf, xs_hbm.at[pl.ds(pl.multiple_of(j * (TME * SLAB), TME * SLAB), TME * SLAB)], zsem)
                    if wait:
                        cp.wait()
                    else:
                        cp.start()
                return carry
            lax.fori_loop(0, N_EBLK, body, 0)

        zero_fill(False)
        zero_fill(True)

    def scatter(wait):
        def body(r, carry):
            for kk in range(TOP_K):
                cp = _slab_copy(h2_ref, xs_hbm, sem, r, dest_ref[0, 0, r * TOP_K + kk])
                if wait:
                    cp.wait()
                else:
                    cp.start(priority=kk % 2)
            return carry
        lax.fori_loop(0, TMC, body, 0, unroll=4)

    scatter(False)
    scatter(True)


def _dispatch(h2, dest3, zero_blk):
    nt = N_TOK // TMC
    grid_spec = pltpu.PrefetchScalarGridSpec(
        num_scalar_prefetch=1,
        grid=(nt,),
        in_specs=[pl.BlockSpec((1, 1, TMC * TOP_K), lambda i, z: (i, 0, 0), memory_space=pltpu.SMEM),
                  pl.BlockSpec((TMC * SLAB, LANES), lambda i, z: (i, 0))],
        out_specs=pl.BlockSpec(memory_space=pl.ANY),
        scratch_shapes=[pltpu.VMEM((TME * SLAB, LANES), F32),
                        pltpu.SemaphoreType.DMA(()),
                        pltpu.SemaphoreType.DMA(())])
    return pl.pallas_call(
        _dispatch_kernel,
        grid_spec=grid_spec,
        out_shape=jax.ShapeDtypeStruct((P_ROWS * SLAB, LANES), F32),
        compiler_params=_params(("arbitrary",)),
        name="moe_dispatch",
    )(zero_blk, dest3, h2)


def _moe_kernel(blk_e_ref, nused_ref, x_ref, win_ref, bin_ref, wout_ref, bout_ref, y_ref, win_bf, wout_bf):
    j = pl.program_id(0)
    n_used = nused_ref[0]

    @pl.when(j < n_used)
    def _():
        changed = jnp.logical_or(j == 0, blk_e_ref[j] != blk_e_ref[jnp.maximum(j - 1, 0)])

        @pl.when(changed)
        def _():
            win_bf[...] = win_ref[0].astype(BF16)
            wout_bf[...] = wout_ref[0].astype(BF16)

        xb = _load_slabs(x_ref, 0, TME).astype(BF16)
        hb = jnp.dot(xb, win_bf[...], preferred_element_type=F32) + bin_ref[0]
        glu = jnp.minimum(hb[:, :D_EXPERT], SWIGLU_LIMIT)
        lin = jnp.clip(hb[:, D_EXPERT:], -SWIGLU_LIMIT, SWIGLU_LIMIT)
        act = glu * jax.nn.sigmoid(SWIGLU_ALPHA * glu) * (lin + 1.0)
        y = jnp.dot(act.astype(BF16), wout_bf[...], preferred_element_type=F32) + bout_ref[0]
        _store_slabs(y_ref, 0, y)

    @pl.when(j >= n_used)
    def _():
        y_ref[...] = jnp.zeros_like(y_ref)


def _moe(xs, blk_e, n_used, w_in, b_in, w_out, b_out, layer):
    e0 = layer * N_EXPERTS
    grid_spec = pltpu.PrefetchScalarGridSpec(
        num_scalar_prefetch=2,
        grid=(N_EBLK,),
        in_specs=[pl.BlockSpec((TME * SLAB, LANES), lambda j, be, nu: (jnp.minimum(j, nu[0] - 1), 0)),
                  pl.BlockSpec((1, D_MODEL, 2 * D_EXPERT), lambda j, be, nu: (e0 + be[j], 0, 0)),
                  pl.BlockSpec((1, 1, 2 * D_EXPERT), lambda j, be, nu: (e0 + be[j], 0, 0)),
                  pl.BlockSpec((1, D_EXPERT, D_MODEL), lambda j, be, nu: (e0 + be[j], 0, 0)),
                  pl.BlockSpec((1, 1, D_MODEL), lambda j, be, nu: (e0 + be[j], 0, 0))],
        out_specs=pl.BlockSpec((TME * SLAB, LANES), lambda j, be, nu: (j, 0)),
        scratch_shapes=[pltpu.VMEM((D_MODEL, 2 * D_EXPERT), BF16),
                        pltpu.VMEM((D_EXPERT, D_MODEL), BF16)])
    return pl.pallas_call(
        _moe_kernel,
        grid_spec=grid_spec,
        out_shape=jax.ShapeDtypeStruct((P_ROWS * SLAB, LANES), F32),
        compiler_params=_params(("arbitrary",)),
        name="moe_ffn",
    )(blk_e, n_used, xs, w_in, b_in, w_out, b_out)


def _combine_tile(tm, dest_ref, destn_ref, y_hbm, xmid_ref, mod_ref, route_ref, buf, sem):
    i = pl.program_id(0)
    n = pl.num_programs(0)
    slot = i % 2

    def gather(idx_ref, s, wait):
        def body(r, carry):
            for kk in range(TOP_K):
                cp = _slab_copy(y_hbm, buf, sem.at[s], idx_ref[0, 0, r * TOP_K + kk],
                                (s * TOP_K + kk) * tm + r)
                if wait:
                    cp.wait()
                else:
                    cp.start(priority=kk % 2)
            return carry
        lax.fori_loop(0, tm, body, 0, unroll=4)

    @pl.when(i == 0)
    def _():
        gather(dest_ref, 0, False)

    @pl.when(i + 1 < n)
    def _():
        gather(destn_ref, 1 - slot, False)

    gather(dest_ref, slot, True)
    gates = route_ref[...]
    acc = gates[:, TOP_K:TOP_K + 1] * _load_slabs(buf, slot * TOP_K * tm, tm)
    for kk in range(1, TOP_K):
        acc = acc + gates[:, TOP_K + kk:TOP_K + kk + 1] * _load_slabs(buf, (slot * TOP_K + kk) * tm, tm)
    return xmid_ref[...] + mod_ref[0][5:6] * acc


def _combine_specs(tm):
    nt = N_TOK // tm
    in_specs = [pl.BlockSpec((1, 1, tm * TOP_K), lambda i: (i, 0, 0), memory_space=pltpu.SMEM),
                pl.BlockSpec((1, 1, tm * TOP_K), lambda i: (jnp.minimum(i + 1, nt - 1), 0, 0),
                             memory_space=pltpu.SMEM),
                pl.BlockSpec(memory_space=pl.ANY),
                pl.BlockSpec((tm, D_MODEL), lambda i: (i, 0)),
                pl.BlockSpec((1, 6, D_MODEL), lambda i: (_seg(i, tm), 0, 0)),
                pl.BlockSpec((tm, LANES), lambda i: (i, 0))]
    scratch = [pltpu.VMEM((2 * TOP_K * tm * SLAB, LANES), F32), pltpu.SemaphoreType.DMA((2,))]
    return in_specs, scratch


def _combine_kernel(final, dest_ref, destn_ref, y_hbm, xmid_ref, mod_ref, route_ref, fg_ref, o_ref, buf, sem):
    out = _combine_tile(TMC, dest_ref, destn_ref, y_hbm, xmid_ref, mod_ref, route_ref, buf, sem)
    o_ref[...] = _rms(out, fg_ref[...]) if final else out


def _combine(y, dest3, xmid, mod, route, final_g, final):
    in_specs, scratch = _combine_specs(TMC)
    return pl.pallas_call(
        functools.partial(_combine_kernel, final),
        grid=(N_TOK // TMC,),
        in_specs=in_specs + [pl.BlockSpec((1, D_MODEL), lambda i: (0, 0))],
        out_specs=pl.BlockSpec((TMC, D_MODEL), lambda i: (i, 0)),
        out_shape=jax.ShapeDtypeStruct((N_TOK, D_MODEL), F32),
        scratch_shapes=scratch,
        compiler_params=_params(("arbitrary",)),
        name="moe_combine_final" if final else "moe_combine",
    )(dest3, dest3, y, xmid, mod, route, final_g.reshape(1, D_MODEL))


def _axial_tables(seq, dim, lanes, offset):
    half = dim // 2
    nf = half // 2
    t = jnp.arange(seq)
    row = (t // GRID_W).astype(F32)
    col = (t % GRID_W).astype(F32)
    inv = ROPE_BASE ** (-jnp.arange(nf, dtype=F32) / nf)
    d = jnp.arange(lanes) - offset
    inside = (d >= 0) & (d < dim)
    dd = jnp.clip(d, 0, dim - 1)
    w = dd % half
    pos = jnp.where((dd // half)[None, :] == 0, row[:, None], col[:, None])
    ang = pos * inv[w % nf][None, :]
    sign = jnp.where(w < nf, -1.0, 1.0)[None, :]
    cos = jnp.where(inside[None, :], jnp.cos(ang), 1.0)
    sins = jnp.where(inside[None, :], sign * jnp.sin(ang), 0.0)
    return cos.astype(F32), sins.astype(F32)


def kernel(x_prompt, x_sample, c, state_ret, state_ssm_re, state_ssm_im, cache_mla_ckv, cache_mla_krope, c_ctx, w_ada, b_ada, norm1_g, norm2_g, final_norm_g, w_in_ab, w_out_ab, ret_decay_logit, ssm_a_re, ssm_a_im, ssm_log_dt, ssm_b_re, ssm_b_im, ssm_c_re, ssm_c_im, ssm_d, ssm_w_glu, w_in_c, mla_q_norm_g, mla_kv_norm_g, mla_w_uq, mla_w_ukv, w_out_c, router_w, router_b, moe_w_in, moe_b_in, moe_w_out, moe_b_out):
    x = jnp.concatenate([x_prompt.reshape(NP, D_MODEL), x_sample.reshape(NS, D_MODEL)], axis=0)
    mods = _modulation(c_ctx, c, w_ada, b_ada)

    rc, rs = _axial_tables(DEC_SEQ, RET_QK_DIM, RET_QK_DIM, 0)
    ret_rope = (jnp.tile(rc, (1, RET_HEADS)), jnp.tile(rs, (1, RET_HEADS)))
    mc, ms = _axial_tables(DEC_SEQ, MLA_ROPE, LANES, MLA_NOPE)
    mla_cos = jnp.concatenate([jnp.ones((TM, LANES), F32), mc], axis=0)
    mla_sin = jnp.concatenate([jnp.zeros((TM, LANES), F32), ms], axis=0)

    w_in_all = moe_w_in.reshape(DEPTH * N_EXPERTS, D_MODEL, 2 * D_EXPERT)
    b_in_all = moe_b_in.reshape(DEPTH * N_EXPERTS, 1, 2 * D_EXPERT)
    w_out_all = moe_w_out.reshape(DEPTH * N_EXPERTS, D_EXPERT, D_MODEL)
    b_out_all = moe_b_out.reshape(DEPTH * N_EXPERTS, 1, D_MODEL)

    s5_w1, s5_w2, s5_a16 = jax.vmap(_s5_tables)(ssm_a_re, ssm_a_im, ssm_log_dt, ssm_b_re, ssm_b_im,
                                                ssm_c_re, ssm_c_im)
    ret_dsum, ret_dec, ret_cd = jax.vmap(_ret_tables)(ret_decay_logit)
    w_in_ab_bf = w_in_ab.astype(BF16)
    w_glu_bf = ssm_w_glu.astype(BF16)
    w_out_bf = jnp.stack([w_out_ab, w_out_c], axis=1).reshape(DEPTH, D_MODEL, D_MODEL).astype(BF16)
    n_c = w_in_c.shape[0]
    cut = MLA_Q_RANK + MLA_KV_RANK
    w_in_c_pad = jnp.concatenate(
        [w_in_c[:, :, :cut], jnp.zeros((n_c, D_MODEL, MLA_NOPE), F32),
         w_in_c[:, :, cut:], jnp.zeros((n_c, D_MODEL, LANES - MLA_NOPE - MLA_ROPE), F32)], axis=2).astype(BF16)
    wq = mla_w_uq.reshape(n_c, MLA_Q_RANK, MLA_HEADS, MLA_NOPE + MLA_ROPE)
    w_uq_pad_all = jnp.pad(wq, ((0, 0), (0, 0), (0, 0), (0, LANES - MLA_NOPE - MLA_ROPE))).reshape(
        n_c, MLA_Q_RANK, MLA_HEADS * LANES).astype(BF16)
    wkv = mla_w_ukv.reshape(n_c, MLA_KV_RANK, MLA_HEADS, MLA_NOPE + MLA_V)
    wk = jnp.pad(wkv[..., :MLA_NOPE], ((0, 0), (0, 0), (0, 0), (0, LANES - MLA_NOPE))).reshape(
        n_c, MLA_KV_RANK, -1)
    wv = wkv[..., MLA_NOPE:].reshape(n_c, MLA_KV_RANK, -1)
    w_ukv_pad_all = jnp.concatenate([wk, wv], axis=2).astype(BF16)
    cache_kr_all = jnp.pad(cache_mla_krope, ((0, 0), (0, 0), (0, 0), (MLA_NOPE, LANES - MLA_NOPE - MLA_ROPE)))

    ret_l, ssm_re_l, ssm_im_l, ckv_l, kr_l = [], [], [], [], []
    mix = jnp.zeros((N_TOK, D_MODEL), BF16)
    for l in range(DEPTH):
        i = l // 2
        mod = mods[l]
        if l % 2 == 0:
            qkvg, u = _pre_ab(x, norm1_g[l], mod, w_in_ab_bf[i])
            w1, w2, a16 = s5_w1[i], s5_w2[i], s5_a16[i]
            x0_p = jnp.zeros((SSM_GROUPS, 2, BATCH, LANES), F32)
            x0_s = jnp.stack(
                [state_ssm_re[:, i].transpose(2, 0, 1, 3).reshape(SSM_GROUPS, DEC_BATCH, LANES),
                 state_ssm_im[:, i].transpose(2, 0, 1, 3).reshape(SSM_GROUPS, DEC_BATCH, LANES)], axis=1)
            yp, xf_p = _s5(u, w1, w2, a16, x0_p, BATCH, SEQ // SSM_CHUNK, 0)
            ys, _ = _s5(u, w1, w2, a16, x0_s, DEC_BATCH, DEC_SEQ // SSM_CHUNK, 1)
            dsum, dec, cd = ret_dsum[i], ret_dec[i], ret_cd[i]
            mix, s_ret = _ab_core(qkvg, u, yp, dsum, dec, cd, ssm_d[i],
                                  w_glu_bf[i], mix, nb=BATCH, seq=SEQ, row0=0)
            (mix,) = _ab_core(qkvg, u, ys, dsum, dec, cd, ssm_d[i],
                              w_glu_bf[i], mix, nb=DEC_BATCH, seq=DEC_SEQ, row0=NP,
                              rope=ret_rope, s0=state_ret[:, i])
            ret_l.append(s_ret)
            xf = xf_p.reshape(SSM_GROUPS, 2, BATCH, 2, SSM_STATE).transpose(1, 2, 3, 0, 4)
            ssm_re_l.append(xf[0])
            ssm_im_l.append(xf[1])
        else:
            w_ukv_pad = w_ukv_pad_all[i]
            q, ckv, kr, krp = _pre_c(x, norm1_g[l], mod, w_in_c_pad[i], mla_q_norm_g[i], mla_kv_norm_g[i],
                                     w_uq_pad_all[i], mla_cos, mla_sin)
            ckv_p = ckv[:NP].reshape(BATCH, SEQ, MLA_KV_RANK)
            kr_p = kr[:NP].reshape(BATCH, SEQ, LANES)
            ckv_s = jnp.concatenate([cache_mla_ckv[:, i], ckv[NP:].reshape(DEC_BATCH, DEC_SEQ, MLA_KV_RANK)],
                                    axis=1)
            kr_s = jnp.concatenate([cache_kr_all[:, i], krp[NP:].reshape(DEC_BATCH, DEC_SEQ, LANES)], axis=1)
            mix = _attention(q, ckv_p, kr_p, w_ukv_pad, mix, nb=BATCH, seq=SEQ, lk=SEQ, row0=0)
            mix = _attention(q, ckv_s, kr_s, w_ukv_pad, mix, nb=DEC_BATCH, seq=DEC_SEQ,
                             lk=PAST_LEN + DEC_SEQ, row0=NP, tq=512)
            ckv_l.append(ckv_p)
            kr_l.append(kr_p[:, :, MLA_NOPE:MLA_NOPE + MLA_ROPE])
        xmid, h2, route, counts = _post(mix, x, mod, norm2_g[l], w_out_bf[l], router_w[l], router_b[l])
        dest, blk_e, n_used, zero_blk = _route_plan(route, counts)
        dest3 = dest.reshape(N_TOK // TMC, 1, TMC * TOP_K)
        xs = _dispatch(h2, dest3, zero_blk)
        y = _moe(xs, blk_e, n_used, w_in_all, b_in_all, w_out_all, b_out_all, l)
        x = _combine(y, dest3, xmid, mod, route, final_norm_g, final=(l == DEPTH - 1))

    y_prompt = x[:NP].reshape(BATCH, SEQ, D_MODEL)
    y_sample = x[NP:].reshape(DEC_BATCH, DEC_SEQ, D_MODEL)
    return (y_prompt, y_sample, jnp.stack(ret_l, axis=1), jnp.stack(ssm_re_l, axis=1),
            jnp.stack(ssm_im_l, axis=1), jnp.stack(ckv_l, axis=1), jnp.stack(kr_l, axis=1))
```

```python
import functools
import math

import jax
import jax.numpy as jnp
import numpy as np
from jax import lax
from jax.experimental import pallas as pl
from jax.experimental.pallas import tpu as pltpu

F32 = jnp.float32
BF16 = jnp.bfloat16
HIGHEST = lax.Precision.HIGHEST

D_MODEL = 1024
BATCH = 32
SEQ = 256
DEPTH = 4
DEC_BATCH = 8
DEC_SEQ = 1024
PAST_LEN = 512
GRID_W = 64
RET_HEADS = 4
RET_V_DIM = 128
RET_QK_DIM = 64
RET_CHUNK = 128
SSM_WIDTH = 512
SSM_GROUP = 16
SSM_GROUPS = 32
SSM_STATE = 64
SSM_CHUNK = 16
MLA_HEADS = 8
MLA_NOPE = 64
MLA_ROPE = 32
MLA_V = 128
MLA_Q_RANK = 256
MLA_KV_RANK = 128
N_EXPERTS = 32
TOP_K = 4
D_EXPERT = 1024
SWIGLU_LIMIT = 7.0
SWIGLU_ALPHA = 1.702
ROPE_BASE = 10000.0
NORM_EPS = 1e-6

NP = BATCH * SEQ
NS = DEC_BATCH * DEC_SEQ
N_TOK = NP + NS
N_SEG = 1 + DEC_BATCH
LANES = 128

TM = 512
TMC = 512
TME = 512
N_ASSIGN = N_TOK * TOP_K
N_EBLK = N_ASSIGN // TME + N_EXPERTS
P_ROWS = N_EBLK * TME
VMEM_LIMIT = 56 * 1024 * 1024


def _params(sem, vmem=VMEM_LIMIT):
    return pltpu.CompilerParams(dimension_semantics=sem, vmem_limit_bytes=vmem)


def _bdot(a, b):
    return jnp.dot(a.astype(BF16), b.astype(BF16), preferred_element_type=F32)


def _split(x):
    hi = x.astype(BF16)
    lo = (x - hi.astype(F32)).astype(BF16)
    return hi, lo


def _dot3(a, b):
    ah, al = _split(a)
    bh, bl = _split(b)
    d = functools.partial(jnp.dot, preferred_element_type=F32)
    return d(ah, bh) + d(al, bh) + d(ah, bl)


def _rms(x, g):
    return x * lax.rsqrt(jnp.mean(x * x, axis=-1, keepdims=True) + NORM_EPS) * g


SLAB = D_MODEL // LANES


def _store_slabs(ref, row0, val):
    m = val.shape[0]
    for s in range(SLAB):
        ref[pl.ds(row0 * SLAB + s, m, stride=SLAB), :] = val[:, s * LANES:(s + 1) * LANES]


def _load_slabs(ref, row0, m):
    return jnp.concatenate([ref[pl.ds(row0 * SLAB + s, m, stride=SLAB), :] for s in range(SLAB)], axis=-1)


def _slab_copy(src, dst, sem, src_row, dst_row):
    return pltpu.make_async_copy(src.at[pl.ds(pl.multiple_of(src_row * SLAB, SLAB), SLAB)],
                                 dst.at[pl.ds(pl.multiple_of(dst_row * SLAB, SLAB), SLAB)], sem)


def _seg(i, tm):
    return jnp.where(i < NP // tm, 0, 1 + (i * tm - NP) // DEC_SEQ)


def _mod_kernel(c_ref, w_ref, b_ref, o_ref):
    c = c_ref[...]
    s = c * jax.nn.sigmoid(c)
    o_ref[0] = _dot3(s, w_ref[0]) + b_ref[0]


def _modulation(c_ctx, c, w_ada, b_ada):
    rows = 16
    cvec = jnp.zeros((rows, D_MODEL), F32).at[0].set(c_ctx).at[1:N_SEG].set(c)
    nb = 1536
    out = pl.pallas_call(
        _mod_kernel,
        grid=(DEPTH, 6 * D_MODEL // nb),
        in_specs=[pl.BlockSpec((rows, D_MODEL), lambda l, j: (0, 0)),
                  pl.BlockSpec((1, D_MODEL, nb), lambda l, j: (l, 0, j)),
                  pl.BlockSpec((1, 1, nb), lambda l, j: (l, 0, j))],
        out_specs=pl.BlockSpec((1, rows, nb), lambda l, j: (l, 0, j)),
        out_shape=jax.ShapeDtypeStruct((DEPTH, rows, 6 * D_MODEL), F32),
        compiler_params=_params(("arbitrary", "arbitrary")),
        name="adaln_mod",
    )(cvec, w_ada, b_ada.reshape(DEPTH, 1, 6 * D_MODEL))
    return out[:, :N_SEG].reshape(DEPTH, N_SEG, 6, D_MODEL)


def _pre_ab_kernel(x_ref, g_ref, mod_ref, w_ref, qkvg_ref, u_ref):
    m = mod_ref[0]
    h = _rms(x_ref[...], g_ref[...]) * (1.0 + m[1:2]) + m[0:1]
    p = _bdot(h, w_ref[...])
    qkvg_ref[...] = p[:, :1536]
    for k in range(SSM_WIDTH // LANES):
        u_ref[k] = p[:, 1536 + k * LANES:1536 + (k + 1) * LANES]


def _pre_ab(x, g1, mod, w_in):
    return pl.pallas_call(
        _pre_ab_kernel,
        grid=(N_TOK // TM,),
        in_specs=[pl.BlockSpec((TM, D_MODEL), lambda i: (i, 0)),
                  pl.BlockSpec((1, D_MODEL), lambda i: (0, 0)),
                  pl.BlockSpec((1, 6, D_MODEL), lambda i: (_seg(i, TM), 0, 0)),
                  pl.BlockSpec((D_MODEL, 2048), lambda i: (0, 0))],
        out_specs=[pl.BlockSpec((TM, 1536), lambda i: (i, 0)),
                   pl.BlockSpec((SSM_WIDTH // LANES, TM, LANES), lambda i: (0, i, 0))],
        out_shape=[jax.ShapeDtypeStruct((N_TOK, 1536), F32),
                   jax.ShapeDtypeStruct((SSM_WIDTH // LANES, N_TOK, LANES), F32)],
        compiler_params=_params(("arbitrary",)),
        name="pre_ab",
    )(x, g1.reshape(1, D_MODEL), mod, w_in)


def _rope_lanes(x, cos, sins, first, shift):
    w = x.shape[-1]
    partner = jnp.where(first, pltpu.roll(x, w - shift, 1), pltpu.roll(x, shift, 1))
    return x * cos + partner * sins


def _pre_c_kernel(x_ref, g_ref, mod_ref, w_ref, qg_ref, kvg_ref, wuq_ref, cos_ref, sin_ref,
                  q_ref, ckv_ref, kr_ref, krp_ref):
    m = mod_ref[0]
    h = _rms(x_ref[...], g_ref[...]) * (1.0 + m[1:2]) + m[0:1]
    p = _bdot(h, w_ref[...])
    cq = _rms(p[:, :MLA_Q_RANK], qg_ref[...])
    ckv_ref[...] = _rms(p[:, MLA_Q_RANK:MLA_Q_RANK + MLA_KV_RANK], kvg_ref[...])
    kr = p[:, MLA_Q_RANK + MLA_KV_RANK:]
    kr_ref[...] = kr
    cos = cos_ref[...]
    sins = sin_ref[...]
    lane = lax.broadcasted_iota(jnp.int32, cos.shape, 1)
    first = (lane % 16) < 8
    krp_ref[...] = _rope_lanes(kr, cos, sins, first, 8)
    q = _bdot(cq, wuq_ref[...])
    for hh in range(MLA_HEADS):
        sl = slice(hh * LANES, (hh + 1) * LANES)
        q_ref[:, sl] = _rope_lanes(q[:, sl], cos, sins, first, 8).astype(BF16)


def _pre_c(x, g1, mod, w_in_pad, q_g, kv_g, w_uq_pad, cos_t, sin_t):
    def tab(i):
        return (jnp.where(i < NP // TM, 0, 1 + ((i * TM - NP) % DEC_SEQ) // TM), 0)
    return pl.pallas_call(
        _pre_c_kernel,
        grid=(N_TOK // TM,),
        in_specs=[pl.BlockSpec((TM, D_MODEL), lambda i: (i, 0)),
                  pl.BlockSpec((1, D_MODEL), lambda i: (0, 0)),
                  pl.BlockSpec((1, 6, D_MODEL), lambda i: (_seg(i, TM), 0, 0)),
                  pl.BlockSpec((D_MODEL, 512), lambda i: (0, 0)),
                  pl.BlockSpec((1, MLA_Q_RANK), lambda i: (0, 0)),
                  pl.BlockSpec((1, MLA_KV_RANK), lambda i: (0, 0)),
                  pl.BlockSpec((MLA_Q_RANK, MLA_HEADS * LANES), lambda i: (0, 0)),
                  pl.BlockSpec((TM, LANES), tab),
                  pl.BlockSpec((TM, LANES), tab)],
        out_specs=[pl.BlockSpec((TM, MLA_HEADS * LANES), lambda i: (i, 0)),
                   pl.BlockSpec((TM, MLA_KV_RANK), lambda i: (i, 0)),
                   pl.BlockSpec((TM, LANES), lambda i: (i, 0)),
                   pl.BlockSpec((TM, LANES), lambda i: (i, 0))],
        out_shape=[jax.ShapeDtypeStruct((N_TOK, MLA_HEADS * LANES), BF16),
                   jax.ShapeDtypeStruct((N_TOK, MLA_KV_RANK), F32),
                   jax.ShapeDtypeStruct((N_TOK, LANES), F32),
                   jax.ShapeDtypeStruct((N_TOK, LANES), F32)],
        compiler_params=_params(("arbitrary",)),
        name="pre_c",
    )(x, g1.reshape(1, D_MODEL), mod, w_in_pad, q_g.reshape(1, -1), kv_g.reshape(1, -1), w_uq_pad,
      cos_t, sin_t)


def _s5_tables(a_re, a_im, log_dt, b_re, b_im, c_re, c_im):
    C = SSM_CHUNK
    a_re = jnp.minimum(a_re.astype(F32), -1e-4)
    a_im = a_im.astype(F32)
    dt = jnp.exp(log_dt.astype(F32))[..., None]
    mag = jnp.exp(dt * a_re)
    ab_re = mag * jnp.cos(dt * a_im)
    ab_im = mag * jnp.sin(dt * a_im)
    den = a_re * a_re + a_im * a_im
    f_re = ((ab_re - 1.0) * a_re + ab_im * a_im) / den
    f_im = (ab_im * a_re - (ab_re - 1.0) * a_im) / den
    b_re = b_re.astype(F32)
    b_im = b_im.astype(F32)
    bb_re = f_re[..., None] * b_re - f_im[..., None] * b_im
    bb_im = f_re[..., None] * b_im + f_im[..., None] * b_re
    ks = jnp.arange(C + 1, dtype=F32)[:, None, None, None]
    pmag = jnp.exp(ks * (dt * a_re)[None])
    p_re = pmag * jnp.cos(ks * (dt * a_im)[None])
    p_im = pmag * jnp.sin(ks * (dt * a_im)[None])
    c_re = c_re.astype(F32)[None]
    c_im = c_im.astype(F32)[None]
    cl_re = c_re * p_re[:, :, :, None, :] - c_im * p_im[:, :, :, None, :]
    cl_im = c_re * p_im[:, :, :, None, :] + c_im * p_re[:, :, :, None, :]
    kmat = (jnp.einsum('kdgpn,dgnq->kdgpq', cl_re[:C], bb_re, precision=HIGHEST)
            - jnp.einsum('kdgpn,dgnq->kdgpq', cl_im[:C], bb_im, precision=HIGHEST))
    lag = np.arange(C)[None, :] - np.arange(C)[:, None]
    sel_f = (lag[:, :, None] == np.arange(C)).reshape(C * C, C).astype(np.float32)
    sel_b = (-lag[:, :, None] == np.arange(C)).reshape(C * C, C).astype(np.float32)
    t = (jnp.dot(sel_f, kmat[:, 0].reshape(C, -1), precision=HIGHEST)
         + jnp.dot(sel_b, kmat[:, 1].reshape(C, -1), precision=HIGHEST))
    m_loc = (t.reshape(C, C, SSM_GROUPS, SSM_GROUP, SSM_GROUP).transpose(2, 0, 4, 1, 3)
             .reshape(SSM_GROUPS, C * SSM_GROUP, C * SSM_GROUP))

    def state_in(d, flip):
        pr = (p_re[:C, d][::-1] if flip else p_re[:C, d])[..., None]
        pi = (p_im[:C, d][::-1] if flip else p_im[:C, d])[..., None]
        er = pr * bb_re[d][None] - pi * bb_im[d][None]
        ei = pr * bb_im[d][None] + pi * bb_re[d][None]
        to = lambda e: e.transpose(1, 0, 3, 2).reshape(SSM_GROUPS, C * SSM_GROUP, SSM_STATE)
        return to(er), to(ei)

    ef_re, ef_im = state_in(0, True)
    eb_re, eb_im = state_in(1, False)
    w1 = jnp.concatenate([m_loc, ef_re, eb_re, ef_im, eb_im], axis=-1)

    def state_out(d, flip):
        to = lambda e: e.transpose(1, 3, 0, 2).reshape(SSM_GROUPS, SSM_STATE, C * SSM_GROUP)
        cr = cl_re[1:C + 1, d][::-1] if flip else cl_re[1:C + 1, d]
        ci = cl_im[1:C + 1, d][::-1] if flip else cl_im[1:C + 1, d]
        return to(cr), to(-ci)

    ff_re, ff_im = state_out(0, False)
    fb_re, fb_im = state_out(1, True)
    w2 = jnp.concatenate([ff_re, fb_re, ff_im, fb_im], axis=1)
    a16 = jnp.stack([jnp.concatenate([p_re[C, 0], p_re[C, 1]], axis=-1),
                     jnp.concatenate([p_im[C, 0], p_im[C, 1]], axis=-1)], axis=1)
    return w1.astype(BF16), w2.astype(BF16), a16


S5_GB = LANES // SSM_GROUP


def _s5_kernel(nb, nc, u_ref, w1_ref, w2_ref, a_ref, x0_ref, y_ref, xf_ref,
               ug_ref, yl_ref, lre_ref, lim_ref, s_ref, l2re_ref, l2im_ref):
    r = nb * nc
    for s in range(SSM_CHUNK):
        piece = u_ref[pl.ds(s, r, stride=SSM_CHUNK), :]
        for gl in range(S5_GB):
            ug_ref[gl, :, s * SSM_GROUP:(s + 1) * SSM_GROUP] = piece[:, gl * SSM_GROUP:(gl + 1) * SSM_GROUP]
    fwd = lax.broadcasted_iota(jnp.int32, (nb, LANES), 1) < SSM_STATE

    def group(gl, carry):
        r1 = _bdot(ug_ref[gl], w1_ref[gl])
        yl_ref[gl] = r1[:, 0:256]
        lre_ref[...] = r1[:, 256:384]
        lim_ref[...] = r1[:, 384:512]
        a = a_ref[gl]
        a_re = a[0:1]
        a_im = a[1:2]
        s_re = x0_ref[gl, 0]
        s_im = x0_ref[gl, 1]
        s_ref[0, 0] = s_re
        s_ref[1, 0] = s_im
        for k in range(nc):
            rf = pl.ds(k, nb, stride=nc)
            rb = pl.ds(nc - 1 - k, nb, stride=nc)
            l_re = jnp.where(fwd, lre_ref[rf, :], lre_ref[rb, :])
            l_im = jnp.where(fwd, lim_ref[rf, :], lim_ref[rb, :])
            n_re = a_re * s_re - a_im * s_im + l_re
            n_im = a_re * s_im + a_im * s_re + l_im
            s_re, s_im = n_re, n_im
            s_ref[0, k + 1] = s_re
            s_ref[1, k + 1] = s_im
        xf_ref[gl, 0] = s_re
        xf_ref[gl, 1] = s_im
        for c in range(nc):
            rows = pl.ds(c, nb, stride=nc)
            l2re_ref[rows, :] = jnp.where(fwd, s_ref[0, c], s_ref[0, nc - 1 - c])
            l2im_ref[rows, :] = jnp.where(fwd, s_ref[1, c], s_ref[1, nc - 1 - c])
        l2 = jnp.concatenate([l2re_ref[...], l2im_ref[...]], axis=-1)
        yl_ref[gl] = yl_ref[gl] + _bdot(l2, w2_ref[gl])
        return carry

    lax.fori_loop(0, S5_GB, group, 0)
    for j in range(SSM_CHUNK):
        tile = jnp.concatenate([yl_ref[gl, :, j * SSM_GROUP:(j + 1) * SSM_GROUP] for gl in range(S5_GB)],
                               axis=-1)
        y_ref[pl.ds(j, r, stride=SSM_CHUNK), :] = tile


def _s5(u4, w1, w2, a16, x0, nb, nc, grp):
    r = nb * nc
    t = r * SSM_CHUNK
    nblk = SSM_WIDTH // LANES
    return pl.pallas_call(
        functools.partial(_s5_kernel, nb, nc),
        grid=(nblk,),
        in_specs=[pl.BlockSpec((None, t, LANES), lambda cb: (cb, grp, 0)),
                  pl.BlockSpec((S5_GB, 256, 512), lambda cb: (cb, 0, 0)),
                  pl.BlockSpec((S5_GB, 256, 256), lambda cb: (cb, 0, 0)),
                  pl.BlockSpec((S5_GB, 2, LANES), lambda cb: (cb, 0, 0)),
                  pl.BlockSpec((S5_GB, 2, nb, LANES), lambda cb: (cb, 0, 0, 0))],
        out_specs=[pl.BlockSpec((None, t, LANES), lambda cb: (cb, 0, 0)),
                   pl.BlockSpec((S5_GB, 2, nb, LANES), lambda cb: (cb, 0, 0, 0))],
        out_shape=[jax.ShapeDtypeStruct((nblk, t, LANES), F32),
                   jax.ShapeDtypeStruct((SSM_GROUPS, 2, nb, LANES), F32)],
        scratch_shapes=[pltpu.VMEM((S5_GB, r, 256), F32),
                        pltpu.VMEM((S5_GB, r, 256), F32),
                        pltpu.VMEM((r, LANES), F32),
                        pltpu.VMEM((r, LANES), F32),
                        pltpu.VMEM((2, nc + 1, nb, LANES), F32),
                        pltpu.VMEM((r, LANES), F32),
                        pltpu.VMEM((r, LANES), F32)],
        compiler_params=_params(("arbitrary",)),
        name="s5_chunked",
    )(u4, w1, w2, a16, x0)


def _ret_tables(logit):
    C = RET_CHUNK
    lg = jax.nn.log_sigmoid(logit.astype(F32))
    pos = jnp.arange(C, dtype=F32)
    diff = pos[:, None] - pos[None, :]
    dsum = (jnp.where(diff >= 0, jnp.exp(jnp.maximum(diff, 0.0)[None] * lg[0][:, None, None]), 0.0)
            + jnp.where(diff <= 0, jnp.exp(jnp.maximum(-diff, 0.0)[None] * lg[1][:, None, None]), 0.0))
    kdf = jnp.exp((C - 1 - pos)[:, None] * lg[0][None, :])
    kdb = jnp.exp(pos[:, None] * lg[1][None, :])
    qdf = jnp.exp((pos + 1)[:, None] * lg[0][None, :])
    qdb = jnp.exp((C - pos)[:, None] * lg[1][None, :])
    dec = jnp.stack([jnp.repeat(t, RET_QK_DIM, axis=1) for t in (kdf, kdb, qdf, qdb)])
    cd = jnp.exp(C * lg)
    return dsum, dec, cd


def _ab_core_kernel(seq, use_pos, with_state, *refs):
    it = iter(refs)
    qkvg_ref, u_ref, ycv_ref, dsum_ref, dec_ref, cd_ref, d_ref, wglu_ref = [next(it) for _ in range(8)]
    if use_pos:
        cos_ref, sin_ref = next(it), next(it)
    s0_ref = next(it) if use_pos else None
    next(it)
    mix_ref = next(it)
    st_ref = next(it) if with_state else None
    C = RET_CHUNK
    n = seq // C
    q = qkvg_ref[:, 0:256] * (RET_QK_DIM ** -0.5)
    k = qkvg_ref[:, 256:512]
    if use_pos:
        cos = cos_ref[...]
        sins = sin_ref[...]
        first = (lax.broadcasted_iota(jnp.int32, cos.shape, 1) % 32) < 16
        q = _rope_lanes(q, cos, sins, first, 16)
        k = _rope_lanes(k, cos, sins, first, 16)
    kdf, kdb, qdf, qdb = dec_ref[0], dec_ref[1], dec_ref[2], dec_ref[3]
    for h in range(RET_HEADS):
        qs = slice(h * RET_QK_DIM, (h + 1) * RET_QK_DIM)
        vs = slice(512 + h * RET_V_DIM, 512 + (h + 1) * RET_V_DIM)
        gs = slice(1024 + h * RET_V_DIM, 1024 + (h + 1) * RET_V_DIM)
        qc, kc, vc, kvf, kvb = [], [], [], [], []
        for c in range(n):
            rows = slice(c * C, (c + 1) * C)
            qc.append(q[rows, qs])
            kc.append(k[rows, qs])
            vc.append(qkvg_ref[rows, vs])
            kvf.append(_bdot((kc[c] * kdf[:, qs]).T, vc[c]))
            kvb.append(_bdot((kc[c] * kdb[:, qs]).T, vc[c]))
        if use_pos:
            sf = s0_ref[0, 0, h]
            sb = s0_ref[0, 1, h]
        else:
            sf = jnp.zeros((RET_QK_DIM, RET_V_DIM), F32)
            sb = jnp.zeros((RET_QK_DIM, RET_V_DIM), F32)
        prev_f = []
        for c in range(n):
            prev_f.append(sf)
            sf = cd_ref[0, h] * sf + kvf[c]
        next_b = [None] * n
        for c in range(n - 1, -1, -1):
            next_b[c] = sb
            sb = cd_ref[1, h] * sb + kvb[c]
        if with_state:
            st_ref[0, 0, h] = sf
            st_ref[0, 1, h] = sb
        for c in range(n):
            rows = slice(c * C, (c + 1) * C)
            sc = lax.dot_general(qc[c].astype(BF16), kc[c].astype(BF16), (((1,), (1,)), ((), ())),
                                 preferred_element_type=F32) * dsum_ref[h]
            o = (_bdot(sc, vc[c]) + _bdot(qc[c] * qdf[:, qs], prev_f[c])
                 + _bdot(qc[c] * qdb[:, qs], next_b[c]))
            o = o * lax.rsqrt(jnp.mean(o * o, axis=-1, keepdims=True) + NORM_EPS)
            g = qkvg_ref[rows, gs]
            mix_ref[rows, h * RET_V_DIM:(h + 1) * RET_V_DIM] = (o * (g * jax.nn.sigmoid(g))).astype(BF16)
    nblk = SSM_WIDTH // LANES
    u = jnp.concatenate([u_ref[kb] for kb in range(nblk)], axis=-1)
    ycv = jnp.concatenate([ycv_ref[kb] for kb in range(nblk)], axis=-1)
    y = ycv + d_ref[...] * u
    z = 0.5 * y * (1.0 + jnp.tanh(math.sqrt(2.0 / math.pi) * (y + 0.044715 * (y * y * y))))
    z = z * jax.nn.sigmoid(_bdot(z, wglu_ref[...]))
    mix_ref[:, 512:1024] = z.astype(BF16)


def _ab_core(qkvg, u, ycv, dsum, dec, cd, d, w_glu, mix_prev, *, nb, seq, row0, rope=None, s0=None):
    use_pos = rope is not None
    with_state = not use_pos
    blk0 = row0 // seq
    in_specs = [pl.BlockSpec((seq, 1536), lambda b: (blk0 + b, 0)),
                pl.BlockSpec((SSM_WIDTH // LANES, seq, LANES), lambda b: (0, blk0 + b, 0)),
                pl.BlockSpec((SSM_WIDTH // LANES, seq, LANES), lambda b: (0, b, 0)),
                pl.BlockSpec((RET_HEADS, RET_CHUNK, RET_CHUNK), lambda b: (0, 0, 0)),
                pl.BlockSpec((4, RET_CHUNK, 256), lambda b: (0, 0, 0)),
                pl.BlockSpec(memory_space=pltpu.SMEM),
                pl.BlockSpec((1, SSM_WIDTH), lambda b: (0, 0)),
                pl.BlockSpec((SSM_WIDTH, SSM_WIDTH), lambda b: (0, 0))]
    args = [qkvg, u, ycv, dsum, dec, cd, d.reshape(1, SSM_WIDTH), w_glu]
    if use_pos:
        in_specs += [pl.BlockSpec((seq, 256), lambda b: (0, 0)),
                     pl.BlockSpec((seq, 256), lambda b: (0, 0)),
                     pl.BlockSpec((1, 2, RET_HEADS, RET_QK_DIM, RET_V_DIM), lambda b: (b, 0, 0, 0, 0))]
        args += [rope[0], rope[1], s0]
    in_specs.append(pl.BlockSpec(memory_space=pl.ANY))
    args.append(mix_prev)
    out_specs = [pl.BlockSpec((seq, D_MODEL), lambda b: (blk0 + b, 0))]
    out_shape = [jax.ShapeDtypeStruct((N_TOK, D_MODEL), BF16)]
    if with_state:
        out_specs.append(pl.BlockSpec((1, 2, RET_HEADS, RET_QK_DIM, RET_V_DIM), lambda b: (b, 0, 0, 0, 0)))
        out_shape.append(jax.ShapeDtypeStruct((nb, 2, RET_HEADS, RET_QK_DIM, RET_V_DIM), F32))
    return pl.pallas_call(
        functools.partial(_ab_core_kernel, seq, use_pos, with_state),
        grid=(nb,),
        in_specs=in_specs,
        out_specs=out_specs,
        out_shape=out_shape,
        input_output_aliases={len(args) - 1: 0},
        compiler_params=_params(("arbitrary",)),
        name="ab_core_pos" if use_pos else "ab_core",
    )(*args)


def _attn_kernel(lk, tq, q_ref, ckv_ref, kr_ref, wukv_ref, _, o_ref, k_sc, v_sc):
    @pl.when(pl.program_id(1) == 0)
    def _():
        kv = _bdot(ckv_ref[0], wukv_ref[...])
        kr = kr_ref[0]
        scale = (MLA_NOPE + MLA_ROPE) ** -0.5
        for h in range(MLA_HEADS):
            sl = slice(h * LANES, (h + 1) * LANES)
            k_sc[:, sl] = ((kv[:, sl] + kr) * scale).astype(BF16)
        v_sc[...] = kv[:, MLA_HEADS * LANES:].astype(BF16)

    for h in range(MLA_HEADS):
        sl = slice(h * LANES, (h + 1) * LANES)
        s = lax.dot_general(q_ref[:, sl], k_sc[:, sl], (((1,), (1,)), ((), ())),
                            preferred_element_type=F32)
        e = jnp.exp(s - jnp.max(s, axis=-1, keepdims=True))
        o = _bdot(e, v_sc[:, sl]) / jnp.sum(e, axis=-1, keepdims=True)
        o_ref[:, sl] = o.astype(BF16)


def _attention(q, ckv_all, kr_all, w_ukv_pad, o_prev, *, nb, seq, lk, row0, tq=256):
    nq = seq // tq
    blk0 = row0 // tq
    return pl.pallas_call(
        functools.partial(_attn_kernel, lk, tq),
        grid=(nb, nq),
        in_specs=[pl.BlockSpec((tq, MLA_HEADS * LANES), lambda b, i: (blk0 + b * nq + i, 0)),
                  pl.BlockSpec((1, lk, MLA_KV_RANK), lambda b, i: (b, 0, 0)),
                  pl.BlockSpec((1, lk, LANES), lambda b, i: (b, 0, 0)),
                  pl.BlockSpec((MLA_KV_RANK, 2 * MLA_HEADS * LANES), lambda b, i: (0, 0)),
                  pl.BlockSpec(memory_space=pl.ANY)],
        out_specs=pl.BlockSpec((tq, MLA_HEADS * MLA_V), lambda b, i: (blk0 + b * nq + i, 0)),
        out_shape=jax.ShapeDtypeStruct((N_TOK, MLA_HEADS * MLA_V), BF16),
        scratch_shapes=[pltpu.VMEM((lk, MLA_HEADS * LANES), BF16),
                        pltpu.VMEM((lk, MLA_HEADS * MLA_V), BF16)],
        input_output_aliases={4: 0},
        compiler_params=_params(("arbitrary", "arbitrary")),
        name="mla_attn_%d" % lk,
    )(q, ckv_all, kr_all, w_ukv_pad, o_prev)


def _post_kernel(mix_ref, x_ref, mod_ref, g_ref, wo_ref, rw_ref, rb_ref,
                 xmid_ref, h2_ref, route_ref, cnt_ref, base_ref):
    i = pl.program_id(0)

    @pl.when(i == 0)
    def _():
        base_ref[...] = jnp.zeros_like(base_ref)

    m = mod_ref[0]
    xm = x_ref[...] + m[2:3] * jnp.dot(mix_ref[...], wo_ref[...], preferred_element_type=F32)
    xmid_ref[...] = xm
    h2 = _rms(xm, g_ref[...]) * (1.0 + m[4:5]) + m[3:4]
    _store_slabs(h2_ref, 0, h2)
    logits = _dot3(h2, rw_ref[...]) + rb_ref[...]
    lane_f = lax.broadcasted_iota(jnp.int32, logits.shape, 1).astype(F32)
    l = logits
    vals, sels = [], []
    for _k in range(TOP_K):
        mx = jnp.max(l, axis=-1, keepdims=True)
        idx = jnp.min(jnp.where(l == mx, lane_f, float(N_EXPERTS)), axis=-1, keepdims=True)
        sel = lane_f == idx
        vals.append(mx)
        sels.append(sel)
        l = jnp.where(sel, -jnp.inf, l)
    ex = [jnp.exp(v - vals[0]) for v in vals]
    den = ex[0] + ex[1] + ex[2] + ex[3]
    onehot = (sels[0] | sels[1] | sels[2] | sels[3]).astype(F32)
    r_i = lax.broadcasted_iota(jnp.int32, (TM, TM), 0)
    c_i = lax.broadcasted_iota(jnp.int32, (TM, TM), 1)
    tri = (c_i < r_i).astype(BF16)
    before = base_ref[...] + jnp.dot(tri, onehot.astype(BF16), preferred_element_type=F32)
    out_lane = lax.broadcasted_iota(jnp.int32, (TM, LANES), 1)
    route = jnp.zeros((TM, LANES), F32)
    for kk in range(TOP_K):
        sel_f = sels[kk].astype(F32)
        eid = jnp.sum(sel_f * lane_f, axis=-1, keepdims=True)
        pos = jnp.sum(sel_f * before, axis=-1, keepdims=True)
        route = jnp.where(out_lane == kk, eid, route)
        route = jnp.where(out_lane == TOP_K + kk, ex[kk] / den, route)
        route = jnp.where(out_lane == 2 * TOP_K + kk, pos, route)
    route_ref[...] = route
    base_ref[...] = base_ref[...] + jnp.sum(onehot, axis=0, keepdims=True)
    cnt_ref[...] = base_ref[...]


def _post(mix, x, mod, g2, w_out, router_w, router_b):
    return pl.pallas_call(
        _post_kernel,
        grid=(N_TOK // TM,),
        in_specs=[pl.BlockSpec((TM, D_MODEL), lambda i: (i, 0)),
                  pl.BlockSpec((TM, D_MODEL), lambda i: (i, 0)),
                  pl.BlockSpec((1, 6, D_MODEL), lambda i: (_seg(i, TM), 0, 0)),
                  pl.BlockSpec((1, D_MODEL), lambda i: (0, 0)),
                  pl.BlockSpec((D_MODEL, D_MODEL), lambda i: (0, 0)),
                  pl.BlockSpec((D_MODEL, N_EXPERTS), lambda i: (0, 0)),
                  pl.BlockSpec((1, N_EXPERTS), lambda i: (0, 0))],
        out_specs=[pl.BlockSpec((TM, D_MODEL), lambda i: (i, 0)),
                   pl.BlockSpec((TM * SLAB, LANES), lambda i: (i, 0)),
                   pl.BlockSpec((TM, LANES), lambda i: (i, 0)),
                   pl.BlockSpec((1, N_EXPERTS), lambda i: (0, 0))],
        out_shape=[jax.ShapeDtypeStruct((N_TOK, D_MODEL), F32),
                   jax.ShapeDtypeStruct((N_TOK * SLAB, LANES), F32),
                   jax.ShapeDtypeStruct((N_TOK, LANES), F32),
                   jax.ShapeDtypeStruct((1, N_EXPERTS), F32)],
        scratch_shapes=[pltpu.VMEM((1, N_EXPERTS), F32)],
        compiler_params=_params(("arbitrary",)),
        name="post_router",
    )(mix, x, mod, g2.reshape(1, D_MODEL), w_out, router_w, router_b.reshape(1, N_EXPERTS))


def _route_plan(route, counts):
    eid = route[:, 0:TOP_K].astype(jnp.int32)
    pos = route[:, 2 * TOP_K:3 * TOP_K].astype(jnp.int32)
    cnt = counts[0].astype(jnp.int32)
    blocks = (cnt + TME - 1) // TME
    blk_end = jnp.cumsum(blocks)
    pad_start = (blk_end - blocks) * TME
    experts = jnp.arange(N_EXPERTS, dtype=jnp.int32)
    dest = jnp.sum(jnp.where(eid[:, :, None] == experts, pad_start, 0), axis=-1) + pos
    n_used = blk_end[-1]
    blk = jnp.arange(N_EBLK, dtype=jnp.int32)
    blk_e = jnp.sum((jnp.minimum(blk, n_used - 1)[:, None] >= blk_end[None, :]).astype(jnp.int32), axis=-1)
    blk_e = jnp.minimum(blk_e, N_EXPERTS - 1).astype(jnp.int32)
    is_last = jnp.any((blk[:, None] == blk_end[None, :] - 1) & (blocks[None, :] > 0), axis=-1)
    zero_blk = (is_last | (blk >= n_used)).astype(jnp.int32)
    return dest.astype(jnp.int32), blk_e, n_used.reshape(1).astype(jnp.int32), zero_blk


def _dispatch_kernel(zero_ref, dest_ref, h2_ref, xs_hbm, zbuf, sem, zsem):
    i = pl.program_id(0)

    @pl.when(i == 0)
    def _():
        zbuf[...] = jnp.zeros_like(zbuf)

        def zero_fill(wait):
            def body(j, carry):
                @pl.when(zero_ref[j] != 0)
                def _():
                    cp = pltpu.make_async_copy(
                        zbuf, xs_hbm.at[pl.ds(pl.multiple_of(j * (TME * SLAB), TME * SLAB), TME * SLAB)], zsem)
                    if wait:
                        cp.wait()
                    else:
                        cp.start()
                return carry
            lax.fori_loop(0, N_EBLK, body, 0)

        zero_fill(False)
        zero_fill(True)

    def scatter(wait):
        def body(r, carry):
            for kk in range(TOP_K):
                cp = _slab_copy(h2_ref, xs_hbm, sem, r, dest_ref[0, 0, r * TOP_K + kk])
                if wait:
                    cp.wait()
                else:
                    cp.start(priority=kk % 2)
            return carry
        lax.fori_loop(0, TMC, body, 0, unroll=4)

    scatter(False)
    scatter(True)


def _dispatch(h2, dest3, zero_blk):
    nt = N_TOK // TMC
    grid_spec = pltpu.PrefetchScalarGridSpec(
        num_scalar_prefetch=1,
        grid=(nt,),
        in_specs=[pl.BlockSpec((1, 1, TMC * TOP_K), lambda i, z: (i, 0, 0), memory_space=pltpu.SMEM),
                  pl.BlockSpec((TMC * SLAB, LANES), lambda i, z: (i, 0))],
        out_specs=pl.BlockSpec(memory_space=pl.ANY),
        scratch_shapes=[pltpu.VMEM((TME * SLAB, LANES), F32),
                        pltpu.SemaphoreType.DMA(()),
                        pltpu.SemaphoreType.DMA(())])
    return pl.pallas_call(
        _dispatch_kernel,
        grid_spec=grid_spec,
        out_shape=jax.ShapeDtypeStruct((P_ROWS * SLAB, LANES), F32),
        compiler_params=_params(("arbitrary",)),
        name="moe_dispatch",
    )(zero_blk, dest3, h2)


def _moe_kernel(blk_e_ref, nused_ref, x_ref, win_ref, bin_ref, wout_ref, bout_ref, y_ref, win_bf, wout_bf):
    j = pl.program_id(0)
    n_used = nused_ref[0]

    @pl.when(j < n_used)
    def _():
        changed = jnp.logical_or(j == 0, blk_e_ref[j] != blk_e_ref[jnp.maximum(j - 1, 0)])

        @pl.when(changed)
        def _():
            win_bf[...] = win_ref[0].astype(BF16)
            wout_bf[...] = wout_ref[0].astype(BF16)

        xb = _load_slabs(x_ref, 0, TME).astype(BF16)
        hb = jnp.dot(xb, win_bf[...], preferred_element_type=F32) + bin_ref[0]
        glu = jnp.minimum(hb[:, :D_EXPERT], SWIGLU_LIMIT)
        lin = jnp.clip(hb[:, D_EXPERT:], -SWIGLU_LIMIT, SWIGLU_LIMIT)
        act = glu * jax.nn.sigmoid(SWIGLU_ALPHA * glu) * (lin + 1.0)
        y = jnp.dot(act.astype(BF16), wout_bf[...], preferred_element_type=F32) + bout_ref[0]
        _store_slabs(y_ref, 0, y)

    @pl.when(j >= n_used)
    def _():
        y_ref[...] = jnp.zeros_like(y_ref)


def _moe(xs, blk_e, n_used, w_in, b_in, w_out, b_out, layer):
    e0 = layer * N_EXPERTS
    grid_spec = pltpu.PrefetchScalarGridSpec(
        num_scalar_prefetch=2,
        grid=(N_EBLK,),
        in_specs=[pl.BlockSpec((TME * SLAB, LANES), lambda j, be, nu: (jnp.minimum(j, nu[0] - 1), 0)),
                  pl.BlockSpec((1, D_MODEL, 2 * D_EXPERT), lambda j, be, nu: (e0 + be[j], 0, 0)),
                  pl.BlockSpec((1, 1, 2 * D_EXPERT), lambda j, be, nu: (e0 + be[j], 0, 0)),
                  pl.BlockSpec((1, D_EXPERT, D_MODEL), lambda j, be, nu: (e0 + be[j], 0, 0)),
                  pl.BlockSpec((1, 1, D_MODEL), lambda j, be, nu: (e0 + be[j], 0, 0))],
        out_specs=pl.BlockSpec((TME * SLAB, LANES), lambda j, be, nu: (j, 0)),
        scratch_shapes=[pltpu.VMEM((D_MODEL, 2 * D_EXPERT), BF16),
                        pltpu.VMEM((D_EXPERT, D_MODEL), BF16)])
    return pl.pallas_call(
        _moe_kernel,
        grid_spec=grid_spec,
        out_shape=jax.ShapeDtypeStruct((P_ROWS * SLAB, LANES), F32),
        compiler_params=_params(("arbitrary",)),
        name="moe_ffn",
    )(blk_e, n_used, xs, w_in, b_in, w_out, b_out)


def _combine_tile(tm, dest_ref, destn_ref, y_hbm, xmid_ref, mod_ref, route_ref, buf, sem):
    i = pl.program_id(0)
    n = pl.num_programs(0)
    slot = i % 2

    def gather(idx_ref, s, wait):
        def body(r, carry):
            for kk in range(TOP_K):
                cp = _slab_copy(y_hbm, buf, sem.at[s], idx_ref[0, 0, r * TOP_K + kk],
                                (s * TOP_K + kk) * tm + r)
                if wait:
                    cp.wait()
                else:
                    cp.start(priority=kk % 2)
            return carry
        lax.fori_loop(0, tm, body, 0, unroll=8)

    @pl.when(i == 0)
    def _():
        gather(dest_ref, 0, False)

    @pl.when(i + 1 < n)
    def _():
        gather(destn_ref, 1 - slot, False)

    gather(dest_ref, slot, True)
    gates = route_ref[...]
    acc = gates[:, TOP_K:TOP_K + 1] * _load_slabs(buf, slot * TOP_K * tm, tm)
    for kk in range(1, TOP_K):
        acc = acc + gates[:, TOP_K + kk:TOP_K + kk + 1] * _load_slabs(buf, (slot * TOP_K + kk) * tm, tm)
    return xmid_ref[...] + mod_ref[0][5:6] * acc


def _combine_specs(tm):
    nt = N_TOK // tm
    in_specs = [pl.BlockSpec((1, 1, tm * TOP_K), lambda i: (i, 0, 0), memory_space=pltpu.SMEM),
                pl.BlockSpec((1, 1, tm * TOP_K), lambda i: (jnp.minimum(i + 1, nt - 1), 0, 0),
                             memory_space=pltpu.SMEM),
                pl.BlockSpec(memory_space=pl.ANY),
                pl.BlockSpec((tm, D_MODEL), lambda i: (i, 0)),
                pl.BlockSpec((1, 6, D_MODEL), lambda i: (_seg(i, tm), 0, 0)),
                pl.BlockSpec((tm, LANES), lambda i: (i, 0))]
    scratch = [pltpu.VMEM((2 * TOP_K * tm * SLAB, LANES), F32), pltpu.SemaphoreType.DMA((2,))]
    return in_specs, scratch


def _combine_kernel(final, dest_ref, destn_ref, y_hbm, xmid_ref, mod_ref, route_ref, fg_ref, o_ref, buf, sem):
    out = _combine_tile(TMC, dest_ref, destn_ref, y_hbm, xmid_ref, mod_ref, route_ref, buf, sem)
    o_ref[...] = _rms(out, fg_ref[...]) if final else out


def _combine(y, dest3, xmid, mod, route, final_g, final):
    in_specs, scratch = _combine_specs(TMC)
    return pl.pallas_call(
        functools.partial(_combine_kernel, final),
        grid=(N_TOK // TMC,),
        in_specs=in_specs + [pl.BlockSpec((1, D_MODEL), lambda i: (0, 0))],
        out_specs=pl.BlockSpec((TMC, D_MODEL), lambda i: (i, 0)),
        out_shape=jax.ShapeDtypeStruct((N_TOK, D_MODEL), F32),
        scratch_shapes=scratch,
        compiler_params=_params(("arbitrary",)),
        name="moe_combine_final" if final else "moe_combine",
    )(dest3, dest3, y, xmid, mod, route, final_g.reshape(1, D_MODEL))


def _axial_tables(seq, dim, lanes, offset):
    half = dim // 2
    nf = half // 2
    t = jnp.arange(seq)
    row = (t // GRID_W).astype(F32)
    col = (t % GRID_W).astype(F32)
    inv = ROPE_BASE ** (-jnp.arange(nf, dtype=F32) / nf)
    d = jnp.arange(lanes) - offset
    inside = (d >= 0) & (d < dim)
    dd = jnp.clip(d, 0, dim - 1)
    w = dd % half
    pos = jnp.where((dd // half)[None, :] == 0, row[:, None], col[:, None])
    ang = pos * inv[w % nf][None, :]
    sign = jnp.where(w < nf, -1.0, 1.0)[None, :]
    cos = jnp.where(inside[None, :], jnp.cos(ang), 1.0)
    sins = jnp.where(inside[None, :], sign * jnp.sin(ang), 0.0)
    return cos.astype(F32), sins.astype(F32)


def kernel(x_prompt, x_sample, c, state_ret, state_ssm_re, state_ssm_im, cache_mla_ckv, cache_mla_krope, c_ctx, w_ada, b_ada, norm1_g, norm2_g, final_norm_g, w_in_ab, w_out_ab, ret_decay_logit, ssm_a_re, ssm_a_im, ssm_log_dt, ssm_b_re, ssm_b_im, ssm_c_re, ssm_c_im, ssm_d, ssm_w_glu, w_in_c, mla_q_norm_g, mla_kv_norm_g, mla_w_uq, mla_w_ukv, w_out_c, router_w, router_b, moe_w_in, moe_b_in, moe_w_out, moe_b_out):
    x = jnp.concatenate([x_prompt.reshape(NP, D_MODEL), x_sample.reshape(NS, D_MODEL)], axis=0)
    mods = _modulation(c_ctx, c, w_ada, b_ada)

    rc, rs = _axial_tables(DEC_SEQ, RET_QK_DIM, RET_QK_DIM, 0)
    ret_rope = (jnp.tile(rc, (1, RET_HEADS)), jnp.tile(rs, (1, RET_HEADS)))
    mc, ms = _axial_tables(DEC_SEQ, MLA_ROPE, LANES, MLA_NOPE)
    mla_cos = jnp.concatenate([jnp.ones((TM, LANES), F32), mc], axis=0)
    mla_sin = jnp.concatenate([jnp.zeros((TM, LANES), F32), ms], axis=0)

    w_in_all = moe_w_in.reshape(DEPTH * N_EXPERTS, D_MODEL, 2 * D_EXPERT)
    b_in_all = moe_b_in.reshape(DEPTH * N_EXPERTS, 1, 2 * D_EXPERT)
    w_out_all = moe_w_out.reshape(DEPTH * N_EXPERTS, D_EXPERT, D_MODEL)
    b_out_all = moe_b_out.reshape(DEPTH * N_EXPERTS, 1, D_MODEL)

    s5_w1, s5_w2, s5_a16 = jax.vmap(_s5_tables)(ssm_a_re, ssm_a_im, ssm_log_dt, ssm_b_re, ssm_b_im,
                                                ssm_c_re, ssm_c_im)
    ret_dsum, ret_dec, ret_cd = jax.vmap(_ret_tables)(ret_decay_logit)
    w_in_ab_bf = w_in_ab.astype(BF16)
    w_glu_bf = ssm_w_glu.astype(BF16)
    w_out_bf = jnp.stack([w_out_ab, w_out_c], axis=1).reshape(DEPTH, D_MODEL, D_MODEL).astype(BF16)
    n_c = w_in_c.shape[0]
    cut = MLA_Q_RANK + MLA_KV_RANK
    w_in_c_pad = jnp.concatenate(
        [w_in_c[:, :, :cut], jnp.zeros((n_c, D_MODEL, MLA_NOPE), F32),
         w_in_c[:, :, cut:], jnp.zeros((n_c, D_MODEL, LANES - MLA_NOPE - MLA_ROPE), F32)], axis=2).astype(BF16)
    wq = mla_w_uq.reshape(n_c, MLA_Q_RANK, MLA_HEADS, MLA_NOPE + MLA_ROPE)
    w_uq_pad_all = jnp.pad(wq, ((0, 0), (0, 0), (0, 0), (0, LANES - MLA_NOPE - MLA_ROPE))).reshape(
        n_c, MLA_Q_RANK, MLA_HEADS * LANES).astype(BF16)
    wkv = mla_w_ukv.reshape(n_c, MLA_KV_RANK, MLA_HEADS, MLA_NOPE + MLA_V)
    wk = jnp.pad(wkv[..., :MLA_NOPE], ((0, 0), (0, 0), (0, 0), (0, LANES - MLA_NOPE))).reshape(
        n_c, MLA_KV_RANK, -1)
    wv = wkv[..., MLA_NOPE:].reshape(n_c, MLA_KV_RANK, -1)
    w_ukv_pad_all = jnp.concatenate([wk, wv], axis=2).astype(BF16)
    cache_kr_all = jnp.pad(cache_mla_krope, ((0, 0), (0, 0), (0, 0), (MLA_NOPE, LANES - MLA_NOPE - MLA_ROPE)))

    ret_l, ssm_re_l, ssm_im_l, ckv_l, kr_l = [], [], [], [], []
    mix = jnp.zeros((N_TOK, D_MODEL), BF16)
    for l in range(DEPTH):
        i = l // 2
        mod = mods[l]
        if l % 2 == 0:
            qkvg, u = _pre_ab(x, norm1_g[l], mod, w_in_ab_bf[i])
            w1, w2, a16 = s5_w1[i], s5_w2[i], s5_a16[i]
            x0_p = jnp.zeros((SSM_GROUPS, 2, BATCH, LANES), F32)
            x0_s = jnp.stack(
                [state_ssm_re[:, i].transpose(2, 0, 1, 3).reshape(SSM_GROUPS, DEC_BATCH, LANES),
                 state_ssm_im[:, i].transpose(2, 0, 1, 3).reshape(SSM_GROUPS, DEC_BATCH, LANES)], axis=1)
            yp, xf_p = _s5(u, w1, w2, a16, x0_p, BATCH, SEQ // SSM_CHUNK, 0)
            ys, _ = _s5(u, w1, w2, a16, x0_s, DEC_BATCH, DEC_SEQ // SSM_CHUNK, 1)
            dsum, dec, cd = ret_dsum[i], ret_dec[i], ret_cd[i]
            mix, s_ret = _ab_core(qkvg, u, yp, dsum, dec, cd, ssm_d[i],
                                  w_glu_bf[i], mix, nb=BATCH, seq=SEQ, row0=0)
            (mix,) = _ab_core(qkvg, u, ys, dsum, dec, cd, ssm_d[i],
                              w_glu_bf[i], mix, nb=DEC_BATCH, seq=DEC_SEQ, row0=NP,
                              rope=ret_rope, s0=state_ret[:, i])
            ret_l.append(s_ret)
            xf = xf_p.reshape(SSM_GROUPS, 2, BATCH, 2, SSM_STATE).transpose(1, 2, 3, 0, 4)
            ssm_re_l.append(xf[0])
            ssm_im_l.append(xf[1])
        else:
            w_ukv_pad = w_ukv_pad_all[i]
            q, ckv, kr, krp = _pre_c(x, norm1_g[l], mod, w_in_c_pad[i], mla_q_norm_g[i], mla_kv_norm_g[i],
                                     w_uq_pad_all[i], mla_cos, mla_sin)
            ckv_p = ckv[:NP].reshape(BATCH, SEQ, MLA_KV_RANK)
            kr_p = kr[:NP].reshape(BATCH, SEQ, LANES)
            ckv_s = jnp.concatenate([cache_mla_ckv[:, i], ckv[NP:].reshape(DEC_BATCH, DEC_SEQ, MLA_KV_RANK)],
                                    axis=1)
            kr_s = jnp.concatenate([cache_kr_all[:, i], krp[NP:].reshape(DEC_BATCH, DEC_SEQ, LANES)], axis=1)
            mix = _attention(q, ckv_p, kr_p, w_ukv_pad, mix, nb=BATCH, seq=SEQ, lk=SEQ, row0=0)
            mix = _attention(q, ckv_s, kr_s, w_ukv_pad, mix, nb=DEC_BATCH, seq=DEC_SEQ,
                             lk=PAST_LEN + DEC_SEQ, row0=NP, tq=512)
            ckv_l.append(ckv_p)
            kr_l.append(kr_p[:, :, MLA_NOPE:MLA_NOPE + MLA_ROPE])
        xmid, h2, route, counts = _post(mix, x, mod, norm2_g[l], w_out_bf[l], router_w[l], router_b[l])
        dest, blk_e, n_used, zero_blk = _route_plan(route, counts)
        dest3 = dest.reshape(N_TOK // TMC, 1, TMC * TOP_K)
        xs = _dispatch(h2, dest3, zero_blk)
        y = _moe(xs, blk_e, n_used, w_in_all, b_in_all, w_out_all, b_out_all, l)
        x = _combine(y, dest3, xmid, mod, route, final_norm_g, final=(l == DEPTH - 1))

    y_prompt = x[:NP].reshape(BATCH, SEQ, D_MODEL)
    y_sample = x[NP:].reshape(DEC_BATCH, DEC_SEQ, D_MODEL)
    return (y_prompt, y_sample, jnp.stack(ret_l, axis=1), jnp.stack(ssm_re_l, axis=1),
            jnp.stack(ssm_im_l, axis=1), jnp.stack(ckv_l, axis=1), jnp.stack(kr_l, axis=1))
```
